```python
import math, functools
import jax, jax.numpy as jnp
from jax import lax
import numpy as np

D_MODEL = 4096
BATCH = 4
SEQ = 2048
DEPTH = 4
DEC_BATCH = 128
DEC_SEQ = 4
PAST_LEN = 8192
PAGE_SIZE = 128

EPS = 1e-6
D_FF = 2 * D_MODEL
D_PLE = 256
Q_BLOCK = 128

GM_CHUNK = 128
GM_HEAD_DIM = 128
GM_WIDTH = D_MODEL // 4
GM_HEADS = GM_WIDTH // GM_HEAD_DIM

SSM_HEAD_DIM = 64
SSM_WIDTH = D_MODEL // 4
SSM_HEADS = SSM_WIDTH // SSM_HEAD_DIM
SSM_GROUPS = 2
SSM_STATE = 128
SSM_CONV = 4
SSM_CHUNK = 128
SSM_CONV_CH = SSM_WIDTH + 2 * SSM_GROUPS * SSM_STATE

V_DIM = 128
NOPE_DIM = 128
ROPE_DIM = 64
QK_DIM = NOPE_DIM + ROPE_DIM
MLA_WIDTH = D_MODEL // 2
MLA_HEADS = MLA_WIDTH // V_DIM
Q_RANK = 3 * D_MODEL // 16
KV_RANK = D_MODEL // 16
ROPE_BASE = 10000.0
ATTN_SCALE = QK_DIM ** -0.5

MIX_WIDTH = GM_WIDTH + SSM_WIDTH + MLA_WIDTH
IN_COLS = 2 * GM_WIDTH + SSM_WIDTH + SSM_CONV_CH + SSM_HEADS + Q_RANK + KV_RANK + ROPE_DIM

kernel_name = 'hybrid_sgu_ssd_mla_macaron_step'


def rms_norm(x, g):
    xf = x.astype(jnp.float32)
    y = xf * lax.rsqrt(jnp.mean(xf * xf, axis=-1, keepdims=True) + EPS)
    return (y * g.astype(jnp.float32)).astype(x.dtype)


def rope(x, pos):
    half = x.shape[-1] // 2
    inv = ROPE_BASE ** (-jnp.arange(half, dtype=jnp.float32) / half)
    ang = pos[:, None] * inv[None, :]
    shape = (ang.shape[0],) + (1,) * (x.ndim - 3) + (half,)
    cos = jnp.cos(ang).reshape(shape).astype(x.dtype)
    sin = jnp.sin(ang).reshape(shape).astype(x.dtype)
    x1, x2 = x[..., :half], x[..., half:]
    return jnp.concatenate([x1 * cos - x2 * sin, x2 * cos + x1 * sin], axis=-1)


def swiglu(x, w_gate, w_up, w_down):
    return (jax.nn.silu(x @ w_gate) * (x @ w_up)) @ w_down


def chunk_sgu(u, v, w_s, b_s, g_v):
    bsz, t, _ = u.shape
    L = GM_CHUNK if t % GM_CHUNK == 0 else t
    vn = rms_norm(v.reshape(bsz, t, GM_HEADS, GM_HEAD_DIM), g_v.reshape(GM_HEADS, GM_HEAD_DIM))
    w = jnp.where(jnp.tril(jnp.ones((L, L), bool)), w_s[:, :L, :L], 0).astype(vn.dtype)
    vc = vn.reshape(bsz, t // L, L, GM_HEADS, GM_HEAD_DIM)
    mixed = jnp.einsum('gts,bcsgd->bctgd', w, vc) + b_s[:, :L].T[None, None, :, :, None]
    return u * mixed.reshape(bsz, t, GM_WIDTH), vn.reshape(bsz, t, GM_WIDTH)


def causal_conv(xbc, buf, w, b):
    full = jnp.concatenate([buf.astype(xbc.dtype), xbc], axis=1)
    out = lax.conv_general_dilated(full, w[:, None, :].astype(xbc.dtype), window_strides=(1,), padding='VALID',
                                   dimension_numbers=('NWC', 'WIO', 'NWC'), feature_group_count=xbc.shape[-1])
    return jax.nn.silu(out + b), full[:, full.shape[1] - (SSM_CONV - 1):]


def segsum(a):
    t = a.shape[-1]
    rep = jnp.broadcast_to(a[..., :, None], a.shape + (t,))
    rep = jnp.where(jnp.tril(jnp.ones((t, t), bool), -1), rep, 0.0)
    cs = jnp.cumsum(rep, axis=-2)
    return jnp.where(jnp.tril(jnp.ones((t, t), bool)), cs, -jnp.inf)


def ssd_scan(x, a, b, c, h0):
    bsz, t, h, p = x.shape
    L = SSM_CHUNK if t % SSM_CHUNK == 0 else t
    nc = t // L
    x = x.reshape(bsz, nc, L, h, p)
    b = b.reshape(bsz, nc, L, h, -1)
    c = c.reshape(bsz, nc, L, h, -1)
    a = a.reshape(bsz, nc, L, h).transpose(0, 3, 1, 2)
    a_cs = jnp.cumsum(a, axis=-1)
    cb = jnp.einsum('bclhn,bcshn->bhcls', c, b) * jnp.exp(segsum(a))
    y_diag = jnp.einsum('bhcls,bcshp->bclhp', cb, x)
    decay_st = jnp.exp(a_cs[..., -1:] - a_cs)
    states = jnp.einsum('bclhn,bhcl,bclhp->bchpn', b, decay_st, x)
    states = jnp.concatenate([h0[:, None], states], axis=1)
    a_chunk = jnp.pad(a_cs[..., -1], ((0, 0), (0, 0), (1, 0)))
    new_states = jnp.einsum('bhzc,bchpn->bzhpn', jnp.exp(segsum(a_chunk)), states)
    y_off = jnp.einsum('bclhn,bchpn,bhcl->bclhp', c, new_states[:, :-1], jnp.exp(a_cs))
    return (y_diag + y_off).reshape(bsz, t, h, p), new_states[:, -1]


def ssd_mixer(z, xbc, dt_raw, conv_buf, h0, lp):
    bsz, t, _ = z.shape
    f32 = jnp.float32
    xbc, new_buf = causal_conv(xbc, conv_buf, lp['ssm_conv_w'], lp['ssm_conv_b'])
    xs = xbc[..., :SSM_WIDTH].astype(f32).reshape(bsz, t, SSM_HEADS, SSM_HEAD_DIM)
    bc = xbc[..., SSM_WIDTH:].astype(f32).reshape(bsz, t, 2, SSM_GROUPS, SSM_STATE)
    hpg = SSM_HEADS // SSM_GROUPS
    bm = jnp.repeat(bc[:, :, 0], hpg, axis=2)
    cm = jnp.repeat(bc[:, :, 1], hpg, axis=2)
    dt = jax.nn.softplus(dt_raw.astype(f32) + lp['ssm_dt_bias'].astype(f32))
    a = -jnp.exp(lp['ssm_a_log'].astype(f32))
    y, h_fin = ssd_scan(xs * dt[..., None], dt * a, bm, cm, h0.astype(f32))
    y = y + xs * lp['ssm_d'].astype(f32)[:, None]
    g = (y.reshape(bsz, t, SSM_WIDTH) * jax.nn.silu(z.astype(f32))).reshape(bsz, t, SSM_GROUPS, SSM_WIDTH // SSM_GROUPS)
    g = g * lax.rsqrt(jnp.mean(g * g, axis=-1, keepdims=True) + EPS)
    out = g.reshape(bsz, t, SSM_WIDTH) * lp['ssm_norm'].astype(f32)
    return out.astype(z.dtype), new_buf, h_fin.astype(h0.dtype)


def mla_project(cq, ckv, kr, pos, lp):
    q = jnp.einsum('btq,qhd->bthd', rms_norm(cq, lp['mla_q_norm']), lp['mla_w_uq'])
    q_nope = rms_norm(q[..., :NOPE_DIM], lp['mla_qn_norm'])
    q_rope = rope(rms_norm(q[..., NOPE_DIM:], lp['mla_qr_norm']), pos)
    c = rms_norm(ckv, lp['mla_kv_norm'])
    k_rope = rope(rms_norm(kr, lp['mla_kr_norm']), pos)
    return q_nope, q_rope, c, k_rope


def mla_keys(c, w_uk, g_kn):
    return rms_norm(jnp.einsum('bpr,rhd->bphd', c, w_uk), g_kn)


def mla_scores(q_nope, q_rope, k_nope, k_rope):
    s = jnp.einsum('bthd,bphd->bthp', q_nope, k_nope) + jnp.einsum('bthd,bpd->bthp', q_rope, k_rope)
    return s.astype(jnp.float32) * ATTN_SCALE


def mla_attend_causal(q_nope, q_rope, c, k_rope, w_uk, w_uv, g_kn):
    bsz, t, h, _ = q_nope.shape
    k_nope = mla_keys(c, w_uk, g_kn)
    v = jnp.einsum('bsr,rhd->bshd', c, w_uv)
    nb = t // Q_BLOCK
    qn = q_nope.reshape(bsz, nb, Q_BLOCK, h, NOPE_DIM).swapaxes(0, 1)
    qr = q_rope.reshape(bsz, nb, Q_BLOCK, h, ROPE_DIM).swapaxes(0, 1)
    kpos = jnp.arange(t)

    def block(args):
        qn_b, qr_b, blk = args
        s = mla_scores(qn_b, qr_b, k_nope, k_rope)
        qpos = blk * Q_BLOCK + jnp.arange(Q_BLOCK)
        s = jnp.where((kpos[None, :] <= qpos[:, None])[None, :, None, :], s, -jnp.inf)
        p = jax.nn.softmax(s, axis=-1).astype(v.dtype)
        return jnp.einsum('bqhk,bkhd->bqhd', p, v)

    o = lax.map(block, (qn, qr, jnp.arange(nb)))
    return o.swapaxes(0, 1).reshape(bsz, t, h * V_DIM)


def online_step(carry, s, c):
    m, l, acc = carry
    m_new = jnp.maximum(m, s.max(axis=-1))
    corr = jnp.exp(m - m_new)
    p = jnp.exp(s - m_new[..., None])
    acc = acc * corr[..., None] + jnp.einsum('bthp,bpr->bthr', p, c.astype(jnp.float32))
    return (m_new, l * corr + p.sum(axis=-1), acc)


def mla_attend_paged(q_nope, q_rope, c_new, kr_new, pool_c, pool_r, page_table, layer, w_uk, w_uv, g_kn):
    bsz, t, h, _ = q_nope.shape
    f32 = jnp.float32

    def page(carry, phys):
        c = pool_c[layer, phys]
        kr = pool_r[layer, phys]
        s = mla_scores(q_nope, q_rope, mla_keys(c, w_uk, g_kn), kr)
        return online_step(carry, s, c), None

    init = (jnp.full((bsz, t, h), -jnp.inf, f32), jnp.zeros((bsz, t, h), f32), jnp.zeros((bsz, t, h, KV_RANK), f32))
    carry, _ = lax.scan(page, init, page_table.T)
    s_new = mla_scores(q_nope, q_rope, mla_keys(c_new, w_uk, g_kn), kr_new)
    s_new = jnp.where(jnp.tril(jnp.ones((t, t), bool))[None, :, None, :], s_new, -jnp.inf)
    _, l, acc = online_step(carry, s_new, c_new)
    o_lat = (acc / l[..., None]).astype(c_new.dtype)
    return jnp.einsum('bthr,rhd->bthd', o_lat, w_uv).reshape(bsz, t, h * V_DIM)


def trunk_layer(h, pe, lp, pos, conv_buf, h0, attend):
    h = h + 0.5 * swiglu(rms_norm(h, lp['norm_ffn1']), lp['w_ffn1_gate'], lp['w_ffn1_up'], lp['w_ffn1_down'])
    a = rms_norm(h, lp['norm_mix']) @ lp['w_in']
    offs = np.cumsum([GM_WIDTH, GM_WIDTH, SSM_WIDTH, SSM_CONV_CH, SSM_HEADS, Q_RANK, KV_RANK]).tolist()
    u, v, z, xbc, dt_raw, cq, ckv, kr = jnp.split(a, offs, axis=-1)
    o_gm, v_rows = chunk_sgu(jax.nn.gelu(u), jax.nn.gelu(v), lp['gm_w_s'], lp['gm_b_s'], lp['gm_norm_v'])
    o_ssm, new_buf, h_fin = ssd_mixer(z, xbc, dt_raw, conv_buf, h0, lp)
    q_nope, q_rope, c, k_rope = mla_project(cq, ckv, kr, pos, lp)
    o_mla = attend(q_nope, q_rope, c, k_rope)
    h = h + jnp.concatenate([o_gm, o_ssm, o_mla], axis=-1) @ lp['w_out']
    h = h + 0.5 * swiglu(rms_norm(h, lp['norm_ffn2']), lp['w_ffn2_gate'], lp['w_ffn2_up'], lp['w_ffn2_down'])
    gate = jax.nn.sigmoid(rms_norm(h, lp['norm_ple']) @ lp['w_ple_gate'])
    h = h + gate * (pe @ lp['w_ple_proj'])
    return h, c, k_rope, h_fin, new_buf, v_rows


def setup_inputs(seed: int = 0) -> dict:
    key = jax.random.key(seed)
    keys = iter(jax.random.split(key, 64))
    f32 = jnp.float32

    def nrm(shape, scale=1.0):
        return jax.random.normal(next(keys), shape, f32) * scale

    def gain(shape):
        return 1.0 + 0.05 * jax.random.normal(next(keys), shape, f32)

    n_pages = PAST_LEN // PAGE_SIZE
    used = DEC_BATCH * n_pages
    n_pool = used + max(1, used // 4)
    page_table = jax.random.permutation(next(keys), n_pool)[:used].reshape(DEC_BATCH, n_pages).astype(jnp.int32)
    dt = jnp.exp(jax.random.uniform(next(keys), (DEPTH, SSM_HEADS), f32, math.log(1e-3), math.log(1e-1)))
    dt_bias = dt + jnp.log(-jnp.expm1(-dt))
    a_log = jnp.log(jax.random.uniform(next(keys), (DEPTH, SSM_HEADS), f32, 1.0, 16.0))
    return {
        'x_prompt': nrm((BATCH, SEQ, D_MODEL)),
        'x_sample': nrm((DEC_BATCH, DEC_SEQ, D_MODEL)),
        'cache_mla_latent': nrm((DEPTH, n_pool, PAGE_SIZE, KV_RANK)),
        'cache_mla_rope': nrm((DEPTH, n_pool, PAGE_SIZE, ROPE_DIM)),
        'state_ssm': nrm((DEPTH, DEC_BATCH, SSM_HEADS, SSM_HEAD_DIM, SSM_STATE), 0.1),
        'state_conv': nrm((DEPTH, DEC_BATCH, SSM_CONV - 1, SSM_CONV_CH)),
        'page_table': page_table,
        'p_prompt': nrm((DEPTH, BATCH, SEQ, D_PLE)),
        'p_sample': nrm((DEPTH, DEC_BATCH, DEC_SEQ, D_PLE)),
        'norm_ffn1': gain((DEPTH, D_MODEL)),
        'w_ffn1_gate': nrm((DEPTH, D_MODEL, D_FF), D_MODEL ** -0.5),
        'w_ffn1_up': nrm((DEPTH, D_MODEL, D_FF), D_MODEL ** -0.5),
        'w_ffn1_down': nrm((DEPTH, D_FF, D_MODEL), D_FF ** -0.5),
        'norm_mix': gain((DEPTH, D_MODEL)),
        'w_in': nrm((DEPTH, D_MODEL, IN_COLS), D_MODEL ** -0.5),
        'gm_norm_v': gain((DEPTH, GM_WIDTH)),
        'gm_w_s': nrm((DEPTH, GM_HEADS, GM_CHUNK, GM_CHUNK), GM_CHUNK ** -0.5),
        'gm_b_s': gain((DEPTH, GM_HEADS, GM_CHUNK)),
        'ssm_conv_w': nrm((DEPTH, SSM_CONV, SSM_CONV_CH), SSM_CONV ** -0.5),
        'ssm_conv_b': nrm((DEPTH, SSM_CONV_CH), 0.01),
        'ssm_dt_bias': dt_bias,
        'ssm_a_log': a_log,
        'ssm_d': gain((DEPTH, SSM_HEADS)),
        'ssm_norm': gain((DEPTH, SSM_WIDTH)),
        'mla_q_norm': gain((DEPTH, Q_RANK)),
        'mla_w_uq': nrm((DEPTH, Q_RANK, MLA_HEADS, QK_DIM), Q_RANK ** -0.5),
        'mla_qn_norm': gain((DEPTH, NOPE_DIM)),
        'mla_qr_norm': gain((DEPTH, ROPE_DIM)),
        'mla_kv_norm': gain((DEPTH, KV_RANK)),
        'mla_kr_norm': gain((DEPTH, ROPE_DIM)),
        'mla_w_uk': nrm((DEPTH, KV_RANK, MLA_HEADS, NOPE_DIM), KV_RANK ** -0.5),
        'mla_kn_norm': gain((DEPTH, NOPE_DIM)),
        'mla_w_uv': nrm((DEPTH, KV_RANK, MLA_HEADS, V_DIM), KV_RANK ** -0.5),
        'w_out': nrm((DEPTH, MIX_WIDTH, D_MODEL), MIX_WIDTH ** -0.5),
        'norm_ffn2': gain((DEPTH, D_MODEL)),
        'w_ffn2_gate': nrm((DEPTH, D_MODEL, D_FF), D_MODEL ** -0.5),
        'w_ffn2_up': nrm((DEPTH, D_MODEL, D_FF), D_MODEL ** -0.5),
        'w_ffn2_down': nrm((DEPTH, D_FF, D_MODEL), D_FF ** -0.5),
        'norm_ple': gain((DEPTH, D_MODEL)),
        'w_ple_gate': nrm((DEPTH, D_MODEL, D_MODEL), D_MODEL ** -0.5),
        'w_ple_proj': nrm((DEPTH, D_PLE, D_MODEL), D_PLE ** -0.5),
    }


def reference(x_prompt, x_sample, cache_mla_latent, cache_mla_rope, state_ssm, state_conv, page_table,
              p_prompt, p_sample, norm_ffn1, w_ffn1_gate, w_ffn1_up, w_ffn1_down, norm_mix, w_in,
              gm_norm_v, gm_w_s, gm_b_s, ssm_conv_w, ssm_conv_b, ssm_dt_bias, ssm_a_log, ssm_d, ssm_norm,
              mla_q_norm, mla_w_uq, mla_qn_norm, mla_qr_norm, mla_kv_norm, mla_kr_norm, mla_w_uk,
              mla_kn_norm, mla_w_uv, w_out, norm_ffn2, w_ffn2_gate, w_ffn2_up, w_ffn2_down,
              norm_ple, w_ple_gate, w_ple_proj):
    bp, tp = x_prompt.shape[0], x_prompt.shape[1]
    ts = x_sample.shape[1]
    pos_p = jnp.arange(tp, dtype=jnp.float32)
    pos_s = PAST_LEN + jnp.arange(ts, dtype=jnp.float32)
    conv0 = jnp.zeros((bp, SSM_CONV - 1, SSM_CONV_CH), x_prompt.dtype)
    ssm0 = jnp.zeros((bp, SSM_HEADS, SSM_HEAD_DIM, SSM_STATE), x_prompt.dtype)
    hp, hs = x_prompt, x_sample
    lat_p, rope_p, ssm_p, conv_p = [], [], [], []
    lat_s, rope_s, ssm_s, conv_s, gmv_s = [], [], [], [], []
    for i in range(DEPTH):
        lp = {
            'norm_ffn1': norm_ffn1[i], 'w_ffn1_gate': w_ffn1_gate[i], 'w_ffn1_up': w_ffn1_up[i],
            'w_ffn1_down': w_ffn1_down[i], 'norm_mix': norm_mix[i], 'w_in': w_in[i],
            'gm_norm_v': gm_norm_v[i], 'gm_w_s': gm_w_s[i], 'gm_b_s': gm_b_s[i],
            'ssm_conv_w': ssm_conv_w[i], 'ssm_conv_b': ssm_conv_b[i], 'ssm_dt_bias': ssm_dt_bias[i],
            'ssm_a_log': ssm_a_log[i], 'ssm_d': ssm_d[i], 'ssm_norm': ssm_norm[i],
            'mla_q_norm': mla_q_norm[i], 'mla_w_uq': mla_w_uq[i], 'mla_qn_norm': mla_qn_norm[i],
            'mla_qr_norm': mla_qr_norm[i], 'mla_kv_norm': mla_kv_norm[i], 'mla_kr_norm': mla_kr_norm[i],
            'w_out': w_out[i], 'norm_ffn2': norm_ffn2[i], 'w_ffn2_gate': w_ffn2_gate[i],
            'w_ffn2_up': w_ffn2_up[i], 'w_ffn2_down': w_ffn2_down[i],
            'norm_ple': norm_ple[i], 'w_ple_gate': w_ple_gate[i], 'w_ple_proj': w_ple_proj[i],
        }
        attend_p = functools.partial(mla_attend_causal, w_uk=mla_w_uk[i], w_uv=mla_w_uv[i], g_kn=mla_kn_norm[i])
        attend_s = functools.partial(mla_attend_paged, pool_c=cache_mla_latent, pool_r=cache_mla_rope,
                                     page_table=page_table, layer=i, w_uk=mla_w_uk[i], w_uv=mla_w_uv[i],
                                     g_kn=mla_kn_norm[i])
        hp, c_p, kr_p, hf_p, cb_p, _ = trunk_layer(hp, p_prompt[i], lp, pos_p, conv0, ssm0, attend_p)
        hs, c_s, kr_s, hf_s, cb_s, v_s = trunk_layer(hs, p_sample[i], lp, pos_s, state_conv[i], state_ssm[i], attend_s)
        lat_p.append(c_p); rope_p.append(kr_p); ssm_p.append(hf_p); conv_p.append(cb_p)
        lat_s.append(c_s); rope_s.append(kr_s); ssm_s.append(hf_s); conv_s.append(cb_s); gmv_s.append(v_s)
    return (hp, hs, jnp.stack(lat_p), jnp.stack(rope_p), jnp.stack(ssm_p), jnp.stack(conv_p),
            jnp.stack(lat_s), jnp.stack(rope_s), jnp.stack(ssm_s), jnp.stack(conv_s), jnp.stack(gmv_s))
```

```python
import functools

import numpy as np
import jax
import jax.numpy as jnp
from jax import lax
from jax.experimental import pallas as pl
from jax.experimental.pallas import tpu as pltpu

F32 = jnp.float32
BF16 = jnp.bfloat16

EPS = 1e-6
ROPE_BASE = 10000.0
PAST_LEN = 8192
PAGE_SIZE = 128

LANES = 128
SUBLANES = 8
MIB = 1024 * 1024

GM_HEAD_DIM = 128
GM_CHUNK = 128
SSM_HEAD_DIM = 64
SSM_GROUPS = 2
SSM_STATE = 128
SSM_CONV = 4
SSM_CHUNK = 128
NOPE_DIM = 128
ROPE_DIM = 64
V_DIM = 128
Q_BLOCK = 128

DT_LANE0 = ROPE_DIM


def _cparams(sem, vmem_mib):
    return pltpu.CompilerParams(dimension_semantics=sem, vmem_limit_bytes=int(vmem_mib * MIB))


def _nt(a, b):
    return lax.dot_general(a, b, (((1,), (1,)), ((), ())), preferred_element_type=F32)


def _dot(a, b):
    return jnp.dot(a, b, preferred_element_type=F32)


def _silu(x):
    return x * jax.nn.sigmoid(x)


def _softplus(x):
    return jnp.maximum(x, 0.0) + jnp.log1p(jnp.exp(-jnp.abs(x)))


def _gelu_tanh(x):
    return 0.5 * x * (1.0 + jnp.tanh(np.sqrt(2.0 / np.pi).astype(np.float32) * (x + 0.044715 * (x * x * x))))


def _rms(x, gain):
    return x * lax.rsqrt(jnp.mean(x * x, axis=-1, keepdims=True) + EPS) * gain


def _split3(x):
    hi = x.astype(BF16)
    r1 = x - hi.astype(F32)
    mid = r1.astype(BF16)
    lo = (r1 - mid.astype(F32)).astype(BF16)
    return hi, mid, lo


def _expand_heads(x, e3):
    return _dot(jnp.concatenate(_split3(x), axis=1), e3)


def _ffn_body(x_ref, g_ref, wg_ref, wu_ref, wd_ref, o_ref, xn_ref):
    @pl.when(pl.program_id(1) == 0)
    def _():
        x = x_ref[...]
        xn_ref[...] = _rms(x, g_ref[...]).astype(BF16)
        o_ref[...] = x

    xn = xn_ref[...]
    gate = _dot(xn, wg_ref[...])
    up = _dot(xn, wu_ref[...])
    act = (_silu(gate) * up * 0.5).astype(BF16)
    o_ref[...] += _dot(act, wd_ref[...])


def _ffn(h, g, wg, wu, wd, *, tm, tf):
    m, d = h.shape
    f = wg.shape[1]
    vmem = (tm * d * 4 * 4 + tm * d * 2 + 3 * 2 * d * tf * 2 + 4 * tm * tf * 4) / MIB + 4
    return pl.pallas_call(
        _ffn_body,
        grid=(m // tm, f // tf),
        in_specs=[
            pl.BlockSpec((tm, d), lambda i, j: (i, 0), pipeline_mode=pl.Buffered(1)),
            pl.BlockSpec((1, d), lambda i, j: (0, 0)),
            pl.BlockSpec((d, tf), lambda i, j: (0, j)),
            pl.BlockSpec((d, tf), lambda i, j: (0, j)),
            pl.BlockSpec((tf, d), lambda i, j: (j, 0)),
        ],
        out_specs=pl.BlockSpec((tm, d), lambda i, j: (i, 0)),
        out_shape=jax.ShapeDtypeStruct((m, d), F32),
        scratch_shapes=[pltpu.VMEM((tm, d), BF16)],
        compiler_params=_cparams(("parallel", "arbitrary"), vmem),
        name="ffn",
    )(h, g, wg, wu, wd)


def _norm_mm_body(x_ref, g_ref, w_ref, o_ref, xn_ref):
    @pl.when(pl.program_id(1) == 0)
    def _():
        xn_ref[...] = _rms(x_ref[...], g_ref[...]).astype(BF16)

    o_ref[...] = _dot(xn_ref[...], w_ref[...])


def _norm_matmul(h, g, w, *, tm, tn):
    m, d = h.shape
    n = w.shape[1]
    vmem = (2 * tm * d * 4 + tm * d * 2 + 2 * d * tn * 2 + 3 * tm * tn * 4) / MIB + 4
    return pl.pallas_call(
        _norm_mm_body,
        grid=(m // tm, n // tn),
        in_specs=[
            pl.BlockSpec((tm, d), lambda i, j: (i, 0), pipeline_mode=pl.Buffered(1)),
            pl.BlockSpec((1, d), lambda i, j: (0, 0)),
            pl.BlockSpec((d, tn), lambda i, j: (0, j)),
        ],
        out_specs=pl.BlockSpec((tm, tn), lambda i, j: (i, j)),
        out_shape=jax.ShapeDtypeStruct((m, n), F32),
        scratch_shapes=[pltpu.VMEM((tm, d), BF16)],
        compiler_params=_cparams(("parallel", "arbitrary"), vmem),
        name="in_proj",
    )(h, g, w)


def _out_proj_body(mix_ref, w_ref, res_ref, o_ref):
    o_ref[...] = res_ref[...] + _dot(mix_ref[...], w_ref[...])


def _out_proj(h, mix, w, *, tm, tn):
    m, d = h.shape
    k = mix.shape[1]
    vmem = (2 * tm * k * 2 + 2 * k * tn * 2 + 4 * tm * tn * 4) / MIB + 4
    return pl.pallas_call(
        _out_proj_body,
        grid=(m // tm, d // tn),
        in_specs=[
            pl.BlockSpec((tm, k), lambda i, j: (i, 0)),
            pl.BlockSpec((k, tn), lambda i, j: (0, j)),
            pl.BlockSpec((tm, tn), lambda i, j: (i, j)),
        ],
        out_specs=pl.BlockSpec((tm, tn), lambda i, j: (i, j)),
        out_shape=jax.ShapeDtypeStruct((m, d), F32),
        compiler_params=_cparams(("parallel", "arbitrary"), vmem),
        name="out_proj",
    )(mix, w, h)


def _ple_body(x_ref, g_ref, pe_ref, wg_ref, wp_ref, res_ref, o_ref, xn_ref):
    @pl.when(pl.program_id(1) == 0)
    def _():
        xn_ref[...] = _rms(x_ref[...], g_ref[...]).astype(BF16)

    gate = jax.nn.sigmoid(_dot(xn_ref[...], wg_ref[...]))
    proj = _dot(pe_ref[...].astype(BF16), wp_ref[...])
    o_ref[...] = res_ref[...] + gate * proj


def _ple(h, g, pe, wg, wp, *, tm, tn):
    m, d = h.shape
    dp = pe.shape[1]
    vmem = (2 * tm * d * 4 + tm * d * 2 + 2 * tm * dp * 4 + 2 * d * tn * 2 + 2 * dp * tn * 2 + 8 * tm * tn * 4) / MIB + 4
    return pl.pallas_call(
        _ple_body,
        grid=(m // tm, d // tn),
        in_specs=[
            pl.BlockSpec((tm, d), lambda i, j: (i, 0), pipeline_mode=pl.Buffered(1)),
            pl.BlockSpec((1, d), lambda i, j: (0, 0)),
            pl.BlockSpec((tm, dp), lambda i, j: (i, 0)),
            pl.BlockSpec((d, tn), lambda i, j: (0, j)),
            pl.BlockSpec((dp, tn), lambda i, j: (0, j)),
            pl.BlockSpec((tm, tn), lambda i, j: (i, j)),
        ],
        out_specs=pl.BlockSpec((tm, tn), lambda i, j: (i, j)),
        out_shape=jax.ShapeDtypeStruct((m, d), F32),
        scratch_shapes=[pltpu.VMEM((tm, d), BF16)],
        compiler_params=_cparams(("parallel", "arbitrary"), vmem),
        name="ple",
    )(h, g, pe, wg, wp, h)


def _sgu_body(u_ref, v_ref, gain_ref, w_ref, bias_ref, o_ref, vn_ref, *, heads):
    u = _gelu_tanh(u_ref[...])
    v = _gelu_tanh(v_ref[...])
    for g in range(heads):
        sl = slice(g * GM_HEAD_DIM, (g + 1) * GM_HEAD_DIM)
        vn = _rms(v[:, sl], gain_ref[:, sl])
        vn_ref[:, sl] = vn
        mixed = _dot(w_ref[0, g], vn.astype(BF16)) + bias_ref[0, :, sl]
        o_ref[:, sl] = (u[:, sl] * mixed).astype(o_ref.dtype)


def _sgu(a, gain, wbig, bias, *, rows, n_prompt_blocks):
    m = a.shape[0]
    width = gain.shape[1]
    heads = width // GM_HEAD_DIM
    sel = lambda i: jnp.minimum(i // n_prompt_blocks, 1)
    return pl.pallas_call(
        functools.partial(_sgu_body, heads=heads),
        grid=(m // rows,),
        in_specs=[
            pl.BlockSpec((rows, width), lambda i: (i, 0)),
            pl.BlockSpec((rows, width), lambda i: (i, 1)),
            pl.BlockSpec((1, width), lambda i: (0, 0)),
            pl.BlockSpec((1, heads, rows, rows), lambda i: (sel(i), 0, 0, 0)),
            pl.BlockSpec((1, rows, width), lambda i: (sel(i), 0, 0)),
        ],
        out_specs=[
            pl.BlockSpec((rows, width), lambda i: (i, 0)),
            pl.BlockSpec((rows, width), lambda i: (i, 0)),
        ],
        out_shape=[jax.ShapeDtypeStruct((m, width), BF16), jax.ShapeDtypeStruct((m, width), F32)],
        compiler_params=_cparams(("arbitrary",), 40),
        name="sgu",
    )(a, a, gain, wbig, bias)


def _ssd_common_tail(y, xs, z, dfull, norm, o_ref, width):
    y = y + xs * dfull
    gt = y * _silu(z)
    gw = width // SSM_GROUPS
    for g in range(SSM_GROUPS):
        sl = slice(g * gw, (g + 1) * gw)
        seg = gt[:, sl]
        o_ref[:, sl] = (seg * lax.rsqrt(jnp.mean(seg * seg, axis=-1, keepdims=True) + EPS) * norm[:, sl]).astype(o_ref.dtype)


def _dt_slab(slab_ref, dtb_ref, alog_ref, heads):
    lane = lax.broadcasted_iota(jnp.int32, slab_ref.shape, 1)
    raw = jnp.where((lane >= DT_LANE0) & (lane < DT_LANE0 + heads), slab_ref[...], 0.0)
    dt = _softplus(raw + dtb_ref[...])
    return dt, dt * (-jnp.exp(alog_ref[...]))


def _ssd_prompt_body(z_ref, xbc_ref, slab_ref, cw_ref, cb_ref, dtb_ref, alog_ref, dfull_ref, norm_ref, tril3_ref,
                     e3_ref, o_ref, st_out_ref, conv_out_ref, full_ref, state_ref, y_ref, *, chunk, n_chunks, heads):
    c = pl.program_id(1)
    width = heads * SSM_HEAD_DIM
    hpg = heads // SSM_GROUPS
    gw = hpg * SSM_HEAD_DIM
    tail = SUBLANES

    @pl.when(c == 0)
    def _():
        full_ref[0:tail, :] = jnp.zeros((tail, full_ref.shape[1]), F32)
        state_ref[...] = jnp.zeros(state_ref.shape, F32)

    full_ref[tail:tail + chunk, :] = xbc_ref[...]
    conv = cb_ref[...]
    for k in range(SSM_CONV):
        conv = conv + cw_ref[k:k + 1, :] * full_ref[pl.ds(tail - (SSM_CONV - 1) + k, chunk), :]
    full_ref[0:tail, :] = full_ref[chunk:chunk + tail, :]
    xc = _silu(conv)
    xs = xc[:, :width]
    bm = xc[:, width:width + SSM_GROUPS * SSM_STATE].astype(BF16)
    cm = xc[:, width + SSM_GROUPS * SSM_STATE:].astype(BF16)

    dt, a = _dt_slab(slab_ref, dtb_ref, alog_ref, heads)
    acs = _dot(tril3_ref[...], jnp.concatenate(_split3(a), axis=0))
    acs_t = acs.T
    arem = acs[chunk - 1:chunk, :] - acs
    e3 = e3_ref[...]
    xdt = xs * _expand_heads(dt, e3)
    ecs = jnp.exp(_expand_heads(acs, e3))
    xw_t = (xdt * jnp.exp(_expand_heads(arem, e3))).T.astype(BF16)
    ecol = ecs.T[:, chunk - 1:chunk]
    xdt_b = xdt.astype(BF16)

    row_i = lax.broadcasted_iota(jnp.int32, (chunk, chunk), 0)
    col_i = lax.broadcasted_iota(jnp.int32, (chunk, chunk), 1)
    causal = row_i >= col_i
    for g in range(SSM_GROUPS):
        cg = cm[:, g * SSM_STATE:(g + 1) * SSM_STATE]
        bg = bm[:, g * SSM_STATE:(g + 1) * SSM_STATE]
        cb = _nt(cg, bg)
        st = state_ref[g * hpg:(g + 1) * hpg].reshape(gw, SSM_STATE)
        y_ref[:, g * gw:(g + 1) * gw] = _nt(cg, st.astype(BF16)) * ecs[:, g * gw:(g + 1) * gw]
        for hh in range(hpg):
            h = g * hpg + hh
            col = acs[:, DT_LANE0 + h:DT_LANE0 + h + 1]
            row = acs_t[DT_LANE0 + h:DT_LANE0 + h + 1, :]
            mat = jnp.where(causal, cb * jnp.exp(col - row), 0.0).astype(BF16)
            hs = slice(h * SSM_HEAD_DIM, (h + 1) * SSM_HEAD_DIM)
            y_ref[:, hs] += _dot(mat, xdt_b[:, hs])
        s_new = _dot(xw_t[g * gw:(g + 1) * gw, :], bg)
        state_ref[g * hpg:(g + 1) * hpg] = (st * ecol[g * gw:(g + 1) * gw, :] + s_new).reshape(hpg, SSM_HEAD_DIM, SSM_STATE)

    _ssd_common_tail(y_ref[...], xs, z_ref[...], dfull_ref[...], norm_ref[...], o_ref, width)

    @pl.when(c == n_chunks - 1)
    def _():
        st_out_ref[0] = state_ref[...]
        conv_out_ref[0] = full_ref[0:tail, :]


def _ssd_prompt(a, params, *, batch, seq, col_z, col_xbc, col_slab):
    cw, cb, dtb, alog, dfull, norm, tril3, e3 = params
    width = norm.shape[1]
    heads = width // SSM_HEAD_DIM
    conv_ch = cw.shape[1]
    chunk = SSM_CHUNK
    nc = seq // chunk
    const = lambda shape: pl.BlockSpec(shape, lambda b, c: (0,) * len(shape))
    rowblk = lambda b, c: b * nc + c
    return pl.pallas_call(
        functools.partial(_ssd_prompt_body, chunk=chunk, n_chunks=nc, heads=heads),
        grid=(batch, nc),
        in_specs=[
            pl.BlockSpec((chunk, width), lambda b, c: (rowblk(b, c), col_z // width)),
            pl.BlockSpec((chunk, conv_ch), lambda b, c: (rowblk(b, c), col_xbc // conv_ch)),
            pl.BlockSpec((chunk, LANES), lambda b, c: (rowblk(b, c), col_slab // LANES)),
            const(cw.shape), const(cb.shape), const(dtb.shape), const(alog.shape), const(dfull.shape),
            const(norm.shape), const(tril3.shape), const(e3.shape),
        ],
        out_specs=[
            pl.BlockSpec((chunk, width), lambda b, c: (rowblk(b, c), 0)),
            pl.BlockSpec((1, heads, SSM_HEAD_DIM, SSM_STATE), lambda b, c: (b, 0, 0, 0)),
            pl.BlockSpec((1, SUBLANES, conv_ch), lambda b, c: (b, 0, 0)),
        ],
        out_shape=[
            jax.ShapeDtypeStruct((batch * seq, width), BF16),
            jax.ShapeDtypeStruct((batch, heads, SSM_HEAD_DIM, SSM_STATE), F32),
            jax.ShapeDtypeStruct((batch, SUBLANES, conv_ch), F32),
        ],
        scratch_shapes=[
            pltpu.VMEM((chunk + SUBLANES, conv_ch), F32),
            pltpu.VMEM((heads, SSM_HEAD_DIM, SSM_STATE), F32),
            pltpu.VMEM((chunk, width), F32),
        ],
        compiler_params=_cparams(("parallel", "arbitrary"), 40),
        name="ssd_prompt",
    )(a, a, a, cw, cb, dtb, alog, dfull, norm, tril3, e3)


def _ssd_sample_body(z_ref, xbc_ref, slab_ref, cs_ref, st_ref, cw_ref, cb_ref, dtb_ref, alog_ref, dfull_ref, norm_ref,
                     e3_ref, o_ref, st_out_ref, conv_out_ref, *, rows, steps, heads):
    width = heads * SSM_HEAD_DIM
    hpg = heads // SSM_GROUPS
    gw = hpg * SSM_HEAD_DIM
    nreq = rows // steps
    x = xbc_ref[...]
    cs = cs_ref[...]
    t_of = lambda shape: lax.broadcasted_iota(jnp.int32, shape, 0) % steps

    def back(v, j):
        return v if j == 0 else pltpu.roll(v, j, axis=0)

    def fwd(v, j):
        return v if j == 0 else pltpu.roll(v, rows - j, axis=0)

    tx = t_of(x.shape)
    conv = cb_ref[...] + cw_ref[SSM_CONV - 1:SSM_CONV, :] * x
    for k in range(SSM_CONV - 1):
        j = SSM_CONV - 1 - k
        conv = conv + cw_ref[k:k + 1, :] * jnp.where(tx >= j, back(x, j), fwd(cs, k))
    conv_out_ref[...] = fwd(x, steps - (SSM_CONV - 1))
    xc = _silu(conv)
    xs = xc[:, :width]
    bm = xc[:, width:width + SSM_GROUPS * SSM_STATE]
    cm = xc[:, width + SSM_GROUPS * SSM_STATE:]

    dt, a = _dt_slab(slab_ref, dtb_ref, alog_ref, heads)
    ts = t_of(a.shape)
    acs = a
    arem = jnp.zeros_like(a)
    for j in range(1, steps):
        acs = acs + jnp.where(ts >= j, back(a, j), 0.0)
        arem = arem + jnp.where(ts < steps - j, fwd(a, j), 0.0)
    e3 = e3_ref[...]
    acsx = _expand_heads(acs, e3)
    xdt = xs * _expand_heads(dt, e3)
    ecs = jnp.exp(acsx)
    xw = xdt * jnp.exp(_expand_heads(arem, e3))

    tw = t_of(xs.shape)
    y = jnp.zeros_like(xs)
    for j in range(steps):
        bj = back(bm, j)
        dec = jnp.exp(acsx - back(acsx, j)) * back(xdt, j)
        parts = []
        for g in range(SSM_GROUPS):
            sl = slice(g * SSM_STATE, (g + 1) * SSM_STATE)
            cbj = jnp.sum(cm[:, sl] * bj[:, sl], axis=-1, keepdims=True)
            parts.append(cbj * dec[:, g * gw:(g + 1) * gw])
        y = y + jnp.where(tw >= j, jnp.concatenate(parts, axis=1), 0.0)

    pad = jnp.zeros((LANES - rows, width), F32)
    xw_t = jnp.concatenate([xw, pad], axis=0).T.astype(BF16)
    e_t = jnp.concatenate([ecs, pad], axis=0).T
    cm_b = cm.astype(BF16)
    bm_pad = jnp.concatenate([bm, jnp.zeros((LANES - rows, bm.shape[1]), F32)], axis=0)
    req_y = lax.broadcasted_iota(jnp.int32, (rows, gw), 0) // steps
    req_b = lax.broadcasted_iota(jnp.int32, (LANES, SSM_STATE), 0) // steps
    yoff = [jnp.zeros((rows, gw), F32) for _ in range(SSM_GROUPS)]
    for b in range(nreq):
        last = b * steps + steps - 1
        for g in range(SSM_GROUPS):
            sl = slice(g * SSM_STATE, (g + 1) * SSM_STATE)
            h0 = st_ref[b, g * hpg:(g + 1) * hpg].reshape(gw, SSM_STATE)
            yoff[g] = yoff[g] + jnp.where(req_y == b, _nt(cm_b[:, sl], h0.astype(BF16)), 0.0)
            b_only = jnp.where(req_b == b, bm_pad[:, sl], 0.0).astype(BF16)
            s_new = _dot(xw_t[g * gw:(g + 1) * gw, :], b_only)
            decay = e_t[g * gw:(g + 1) * gw, last:last + 1]
            st_out_ref[b, g * hpg:(g + 1) * hpg] = (h0 * decay + s_new).reshape(hpg, SSM_HEAD_DIM, SSM_STATE)
    y = y + jnp.concatenate(yoff, axis=1) * ecs

    _ssd_common_tail(y, xs, z_ref[...], dfull_ref[...], norm_ref[...], o_ref, width)


def _ssd_sample(a_s, cs4, state, params, *, steps, nreq_blk, col_z, col_xbc, col_slab):
    cw, cb, dtb, alog, dfull, norm, _, e3 = params
    width = norm.shape[1]
    heads = width // SSM_HEAD_DIM
    conv_ch = cw.shape[1]
    m = a_s.shape[0]
    rows = nreq_blk * steps
    const = lambda shape: pl.BlockSpec(shape, lambda i: (0,) * len(shape))
    st_spec = pl.BlockSpec((nreq_blk, heads, SSM_HEAD_DIM, SSM_STATE), lambda i: (i, 0, 0, 0))
    return pl.pallas_call(
        functools.partial(_ssd_sample_body, rows=rows, steps=steps, heads=heads),
        grid=(m // rows,),
        in_specs=[
            pl.BlockSpec((rows, width), lambda i: (i, col_z // width)),
            pl.BlockSpec((rows, conv_ch), lambda i: (i, col_xbc // conv_ch)),
            pl.BlockSpec((rows, LANES), lambda i: (i, col_slab // LANES)),
            pl.BlockSpec((rows, conv_ch), lambda i: (i, 0)),
            st_spec,
            const(cw.shape), const(cb.shape), const(dtb.shape), const(alog.shape), const(dfull.shape),
            const(norm.shape), const(e3.shape),
        ],
        out_specs=[
            pl.BlockSpec((rows, width), lambda i: (i, 0)),
            st_spec,
            pl.BlockSpec((rows, conv_ch), lambda i: (i, 0)),
        ],
        out_shape=[
            jax.ShapeDtypeStruct((m, width), BF16),
            jax.ShapeDtypeStruct(state.shape, F32),
            jax.ShapeDtypeStruct((m, conv_ch), F32),
        ],
        compiler_params=_cparams(("parallel",), 48),
        name="ssd_sample",
    )(a_s, a_s, a_s, cs4, state, cw, cb, dtb, alog, dfull, norm, e3)


def _rms_pairs(x, bd):
    s = x * x
    hi = s.astype(BF16)
    lo = (s - hi.astype(F32)).astype(BF16)
    ssq = _dot(hi, bd) + _dot(lo, bd)
    return x * lax.rsqrt(ssq * (1.0 / ROPE_DIM) + EPS)


def _rope_pairs(y, cos_t, sin_s):
    lane = lax.broadcasted_iota(jnp.int32, y.shape, 1)
    half = ROPE_DIM // 2
    rot = jnp.where((lane % ROPE_DIM) < half, pltpu.roll(y, LANES - half, axis=1), pltpu.roll(y, half, axis=1))
    return y * cos_t + rot * sin_s


def _mla_proj_body(cq_ref, ckv_ref, slab_ref, cos_ref, sin_ref, gq_ref, wuq_ref, gqn_ref, gqr_ref, gkv_ref, gkr_ref,
                   bd_ref, qn_ref, qr_ref, c_ref, kr_ref, kr2_ref, *, heads):
    cqn = _rms(cq_ref[...], gq_ref[...]).astype(BF16)
    q = _dot(cqn, wuq_ref[...])
    gqn = gqn_ref[...]
    for h in range(heads):
        sl = slice(h * NOPE_DIM, (h + 1) * NOPE_DIM)
        qn_ref[:, sl] = _rms(q[:, sl], gqn).astype(BF16)
    bd = bd_ref[...]
    cos_t = cos_ref[...]
    sin_s = sin_ref[...]
    base = heads * NOPE_DIM
    for j in range(heads * ROPE_DIM // LANES):
        x = q[:, base + j * LANES: base + (j + 1) * LANES]
        y = _rope_pairs(_rms_pairs(x, bd) * gqr_ref[...], cos_t, sin_s)
        qr_ref[:, j * LANES:(j + 1) * LANES] = y.astype(BF16)
    c_ref[...] = _rms(ckv_ref[...], gkv_ref[...])
    kr = _rope_pairs(_rms_pairs(slab_ref[...], bd) * gkr_ref[...], cos_t, sin_s)
    kr_ref[...] = kr
    kr2_ref[...] = (kr + pltpu.roll(kr, ROPE_DIM, axis=1)).astype(BF16)


def _mla_proj(a, cos_t, sin_s, params, *, tm, col_cq, col_ckv, col_slab):
    gq, wuq, gqn, gqr, gkv, gkr, bd = params
    m = a.shape[0]
    q_rank = gq.shape[1]
    kv_rank = gkv.shape[1]
    heads = wuq.shape[1] // (NOPE_DIM + ROPE_DIM)
    const = lambda shape: pl.BlockSpec(shape, lambda i: (0,) * len(shape))
    row = lambda w: pl.BlockSpec((tm, w), lambda i: (i, 0))
    return pl.pallas_call(
        functools.partial(_mla_proj_body, heads=heads),
        grid=(m // tm,),
        in_specs=[
            pl.BlockSpec((tm, q_rank), lambda i: (i, col_cq // q_rank)),
            pl.BlockSpec((tm, kv_rank), lambda i: (i, col_ckv // kv_rank)),
            pl.BlockSpec((tm, LANES), lambda i: (i, col_slab // LANES)),
            row(LANES), row(LANES),
            const(gq.shape), const(wuq.shape), const(gqn.shape), const(gqr.shape), const(gkv.shape),
            const(gkr.shape), const(bd.shape),
        ],
        out_specs=[row(heads * NOPE_DIM), row(heads * ROPE_DIM), row(kv_rank), row(LANES), row(LANES)],
        out_shape=[
            jax.ShapeDtypeStruct((m, heads * NOPE_DIM), BF16),
            jax.ShapeDtypeStruct((m, heads * ROPE_DIM), BF16),
            jax.ShapeDtypeStruct((m, kv_rank), F32),
            jax.ShapeDtypeStruct((m, LANES), F32),
            jax.ShapeDtypeStruct((m, LANES), BF16),
        ],
        compiler_params=_cparams(("parallel",), 48),
        name="mla_proj",
    )(a, a, a, cos_t, sin_s, gq, wuq, gqn, gqr, gkv, gkr, bd)


def _kv_proj_body(c_ref, wuk_ref, wuv_ref, gkn_ref, k_ref, v_ref, *, heads):
    cb = c_ref[...].astype(BF16)
    k = _dot(cb, wuk_ref[...])
    gkn = gkn_ref[...]
    for h in range(heads):
        sl = slice(h * NOPE_DIM, (h + 1) * NOPE_DIM)
        k_ref[:, sl] = _rms(k[:, sl], gkn).astype(BF16)
    v_ref[...] = _dot(cb, wuv_ref[...]).astype(BF16)


def _kv_proj(c, wuk, wuv, gkn, *, rows, tk):
    kv_rank = c.shape[1]
    n = wuk.shape[1]
    heads = n // NOPE_DIM
    const = lambda shape: pl.BlockSpec(shape, lambda i: (0,) * len(shape))
    return pl.pallas_call(
        functools.partial(_kv_proj_body, heads=heads),
        grid=(rows // tk,),
        in_specs=[pl.BlockSpec((tk, kv_rank), lambda i: (i, 0)), const(wuk.shape), const(wuv.shape), const(gkn.shape)],
        out_specs=[pl.BlockSpec((tk, n), lambda i: (i, 0)), pl.BlockSpec((tk, n), lambda i: (i, 0))],
        out_shape=[jax.ShapeDtypeStruct((rows, n), BF16), jax.ShapeDtypeStruct((rows, n), BF16)],
        compiler_params=_cparams(("parallel",), 32),
        name="kv_proj",
    )(c, wuk, wuv, gkn)


def _flash_body(qn_ref, qr_ref, k_ref, kr_ref, v_ref, o_ref, m_ref, l_ref, acc_ref, *, blk, scale):
    h = pl.program_id(1)
    i = pl.program_id(2)
    qn = qn_ref[...]
    lane = lax.broadcasted_iota(jnp.int32, qr_ref.shape, 1)
    qr = jnp.where((lane // ROPE_DIM) == (h % 2), qr_ref[...], jnp.zeros_like(qr_ref[...]))
    m_ref[...] = jnp.full(m_ref.shape, -jnp.inf, F32)
    l_ref[...] = jnp.zeros(l_ref.shape, F32)
    acc_ref[...] = jnp.zeros(acc_ref.shape, F32)

    def step(j, masked):
        rows = pl.ds(pl.multiple_of(j * blk, blk), blk)
        s = (_nt(qn, k_ref[rows, :]) + _nt(qr, kr_ref[rows, :])) * scale
        if masked:
            r = lax.broadcasted_iota(jnp.int32, s.shape, 0)
            c = lax.broadcasted_iota(jnp.int32, s.shape, 1)
            s = jnp.where(c <= r, s, -jnp.inf)
        m_old = m_ref[...]
        m_new = jnp.maximum(m_old, jnp.max(s, axis=-1, keepdims=True))
        corr = jnp.exp(m_old - m_new)
        p = jnp.exp(s - m_new)
        l_ref[...] = l_ref[...] * corr + jnp.sum(p, axis=-1, keepdims=True)
        acc_ref[...] = acc_ref[...] * corr + _dot(p.astype(BF16), v_ref[rows, :])
        m_ref[...] = m_new

    def body(j, carry):
        step(j, False)
        return carry

    lax.fori_loop(0, i, body, 0)
    step(i, True)
    o_ref[...] = (acc_ref[...] / l_ref[...]).astype(o_ref.dtype)


def _flash(qn, qr, k, kr2, v, *, batch, seq, blk, scale):
    heads = k.shape[1] // NOPE_DIM
    nb = seq // blk
    return pl.pallas_call(
        functools.partial(_flash_body, blk=blk, scale=scale),
        grid=(batch, heads, nb),
        in_specs=[
            pl.BlockSpec((blk, NOPE_DIM), lambda b, h, i: (b * nb + i, h)),
            pl.BlockSpec((blk, LANES), lambda b, h, i: (b * nb + i, h // 2)),
            pl.BlockSpec((seq, NOPE_DIM), lambda b, h, i: (b, h)),
            pl.BlockSpec((seq, LANES), lambda b, h, i: (b, 0)),
            pl.BlockSpec((seq, V_DIM), lambda b, h, i: (b, h)),
        ],
        out_specs=pl.BlockSpec((blk, V_DIM), lambda b, h, i: (b * nb + i, h)),
        out_shape=jax.ShapeDtypeStruct((batch * seq, heads * V_DIM), BF16),
        scratch_shapes=[pltpu.VMEM((blk, 1), F32), pltpu.VMEM((blk, 1), F32), pltpu.VMEM((blk, V_DIM), F32)],
        compiler_params=_cparams(("parallel", "parallel", "arbitrary"), 32),
        name="flash",
    )(qn, qr, k, kr2, v)


def _absorb_body(qn_ref, w_ref, o_ref):
    o_ref[0] = _dot(qn_ref[...], w_ref[0]).astype(o_ref.dtype)


def _absorb(qn, w, *, row0, rows):
    heads, _, kv_rank = w.shape
    return pl.pallas_call(
        _absorb_body,
        grid=(heads,),
        in_specs=[
            pl.BlockSpec((rows, NOPE_DIM), lambda h: (row0 // rows, h)),
            pl.BlockSpec((1, NOPE_DIM, kv_rank), lambda h: (h, 0, 0)),
        ],
        out_specs=pl.BlockSpec((1, rows, kv_rank), lambda h: (h, 0, 0)),
        out_shape=jax.ShapeDtypeStruct((heads, rows, kv_rank), F32),
        compiler_params=_cparams(("parallel",), 16),
        name="absorb",
    )(qn, w)


def _v_up_body(o_ref_in, w_ref, o_ref):
    o_ref[...] = _dot(o_ref_in[0].astype(BF16), w_ref[0]).astype(o_ref.dtype)


def _v_up(o_lat, w):
    heads, rows, kv_rank = o_lat.shape
    return pl.pallas_call(
        _v_up_body,
        grid=(heads,),
        in_specs=[
            pl.BlockSpec((1, rows, kv_rank), lambda h: (h, 0, 0)),
            pl.BlockSpec((1, kv_rank, V_DIM), lambda h: (h, 0, 0)),
        ],
        out_specs=pl.BlockSpec((rows, V_DIM), lambda h: (0, h)),
        out_shape=jax.ShapeDtypeStruct((rows, heads * V_DIM), BF16),
        compiler_params=_cparams(("parallel",), 16),
        name="v_up",
    )(o_lat, w)


def _paged_body(pt_ref, qp_ref, qr_ref, cnew_ref, krnew_ref, wukt_ref, *rest, pages, heads, steps, sub, scale):
    c_refs = rest[:pages]
    kr_refs = rest[pages:2 * pages]
    o_ref, m_ref, l_ref, acc_ref, cpad_ref, krpad_ref = rest[2 * pages:]
    j = pl.program_id(1)
    nj = pl.num_programs(1)
    qrows = heads * SUBLANES
    qp = qp_ref[0].reshape(qrows, qp_ref.shape[-1]).astype(BF16)
    qr = qr_ref[0].reshape(qrows, qr_ref.shape[-1]).astype(BF16)
    wukt = wukt_ref[...]

    @pl.when(j == 0)
    def _():
        m_ref[...] = jnp.full(m_ref.shape, -jnp.inf, F32)
        l_ref[...] = jnp.zeros(l_ref.shape, F32)
        acc_ref[...] = jnp.zeros(acc_ref.shape, F32)

    def attend(c_blk, kr_blk, valid):
        kt = _nt(wukt, c_blk)
        parts = []
        for h in range(heads):
            blk = kt[h * NOPE_DIM:(h + 1) * NOPE_DIM, :]
            ssq = jnp.sum(blk * blk, axis=0, keepdims=True)
            parts.append(jnp.broadcast_to(ssq, (SUBLANES, sub)))
        rinv = lax.rsqrt(jnp.concatenate(parts, axis=0) * (1.0 / NOPE_DIM) + EPS)
        s = (_nt(qp, c_blk) * rinv + _nt(qr, kr_blk)) * scale
        if valid is not None:
            s = jnp.where(valid, s, -jnp.inf)
        m_old = m_ref[...]
        m_new = jnp.maximum(m_old, jnp.max(s, axis=-1, keepdims=True))
        corr = jnp.exp(m_old - m_new)
        p = jnp.exp(s - m_new)
        l_ref[...] = l_ref[...] * corr + jnp.sum(p, axis=-1, keepdims=True)
        acc_ref[...] = acc_ref[...] * corr + _dot(p.astype(BF16), c_blk)
        m_ref[...] = m_new

    per = sub // PAGE_SIZE
    for g in range(pages // per):
        c_blk = jnp.concatenate([c_refs[g * per + k][0, 0] for k in range(per)], axis=0).astype(BF16)
        kr_blk = jnp.concatenate([kr_refs[g * per + k][0, 0] for k in range(per)], axis=0).astype(BF16)
        attend(c_blk, kr_blk, None)

    @pl.when(j == nj - 1)
    def _():
        cpad_ref[...] = jnp.zeros(cpad_ref.shape, F32)
        krpad_ref[...] = jnp.zeros(krpad_ref.shape, F32)
        cpad_ref[0:steps, :] = cnew_ref[0]
        krpad_ref[0:steps, :] = krnew_ref[0]
        t = lax.broadcasted_iota(jnp.int32, (qrows, sub), 0) % SUBLANES
        p = lax.broadcasted_iota(jnp.int32, (qrows, sub), 1)
        attend(cpad_ref[...].astype(BF16), krpad_ref[:, :ROPE_DIM].astype(BF16), (p < steps) & (p <= t))
        o_ref[0] = acc_ref[...] / l_ref[...]


def _paged(page_table, qp, qr, c_new, kr_new, wukt, pool_c, pool_r, *, layer, pages, scale):
    nreq, heads, _, kv_rank = qp.shape
    steps = c_new.shape[1]
    n_pages = page_table.shape[1]
    sub = 2 * PAGE_SIZE
    qrows = heads * SUBLANES
    flat_pt = page_table.reshape(-1)

    def page_spec(width, k):
        return pl.BlockSpec((1, 1, PAGE_SIZE, width),
                            lambda b, j, pt: (layer, pt[b * n_pages + j * pages + k], 0, 0))

    grid_spec = pltpu.PrefetchScalarGridSpec(
        num_scalar_prefetch=1,
        grid=(nreq, n_pages // pages),
        in_specs=[
            pl.BlockSpec((1, heads, SUBLANES, kv_rank), lambda b, j, pt: (b, 0, 0, 0)),
            pl.BlockSpec((1, heads, SUBLANES, ROPE_DIM), lambda b, j, pt: (b, 0, 0, 0)),
            pl.BlockSpec((1, steps, kv_rank), lambda b, j, pt: (b, 0, 0)),
            pl.BlockSpec((1, steps, LANES), lambda b, j, pt: (b, 0, 0)),
            pl.BlockSpec(wukt.shape, lambda b, j, pt: (0, 0)),
        ] + [page_spec(kv_rank, k) for k in range(pages)] + [page_spec(ROPE_DIM, k) for k in range(pages)],
        out_specs=pl.BlockSpec((1, qrows, kv_rank), lambda b, j, pt: (b, 0, 0)),
        scratch_shapes=[
            pltpu.VMEM((qrows, 1), F32), pltpu.VMEM((qrows, 1), F32), pltpu.VMEM((qrows, kv_rank), F32),
            pltpu.VMEM((sub, kv_rank), F32), pltpu.VMEM((sub, LANES), F32),
        ],
    )
    return pl.pallas_call(
        functools.partial(_paged_body, pages=pages, heads=heads, steps=steps, sub=sub, scale=scale),
        grid_spec=grid_spec,
        out_shape=jax.ShapeDtypeStruct((nreq, qrows, kv_rank), F32),
        compiler_params=_cparams(("parallel", "arbitrary"), 40),
        name="paged",
    )(flat_pt, qp, qr, c_new, kr_new, wukt, *([pool_c] * pages), *([pool_r] * pages))


def _rope_tables(positions):
    half = ROPE_DIM // 2
    inv = ROPE_BASE ** (-jnp.arange(half, dtype=F32) / half)
    ang = positions[:, None] * inv[None, :]
    cos = jnp.cos(ang).astype(F32)
    sin = jnp.sin(ang).astype(F32)
    reps = LANES // ROPE_DIM
    return jnp.tile(cos, (1, 2 * reps)), jnp.tile(jnp.concatenate([-sin, sin], axis=1), (1, reps))


def _slab(vec, lane0):
    return jnp.zeros((1, LANES), F32).at[0, lane0:lane0 + vec.shape[0]].set(vec.astype(F32))


def kernel(x_prompt, x_sample, cache_mla_latent, cache_mla_rope, state_ssm, state_conv, page_table, p_prompt, p_sample, norm_ffn1, w_ffn1_gate, w_ffn1_up, w_ffn1_down, norm_mix, w_in, gm_norm_v, gm_w_s, gm_b_s, ssm_conv_w, ssm_conv_b, ssm_dt_bias, ssm_a_log, ssm_d, ssm_norm, mla_q_norm, mla_w_uq, mla_qn_norm, mla_qr_norm, mla_kv_norm, mla_kr_norm, mla_w_uk, mla_kn_norm, mla_w_uv, w_out, norm_ffn2, w_ffn2_gate, w_ffn2_up, w_ffn2_down, norm_ple, w_ple_gate, w_ple_proj):
    bp, tp, d_model = x_prompt.shape
    bs, ts, _ = x_sample.shape
    depth = norm_ffn1.shape[0]
    mp, ms = bp * tp, bs * ts
    gm_width = gm_norm_v.shape[1]
    gm_heads = gm_width // GM_HEAD_DIM
    ssm_width = ssm_norm.shape[1]
    ssm_heads = ssm_width // SSM_HEAD_DIM
    conv_ch = ssm_conv_w.shape[2]
    q_rank = mla_q_norm.shape[1]
    kv_rank = mla_kv_norm.shape[1]
    mla_heads = mla_w_uq.shape[2]
    scale = float((NOPE_DIM + ROPE_DIM) ** -0.5)

    tm = ms
    col_v = gm_width
    col_z = 2 * gm_width
    col_xbc = col_z + ssm_width
    col_cq = col_xbc + conv_ch
    col_ckv = col_cq + q_rank
    col_slab = col_ckv + kv_rank
    n_in = col_slab + LANES
    dt0 = col_xbc + conv_ch
    assert col_v == gm_width and col_z % ssm_width == 0 and col_xbc % conv_ch == 0
    assert col_cq % q_rank == 0 and col_ckv % kv_rank == 0 and col_slab % LANES == 0
    assert mp % tm == 0 and ms == tm and tp % SSM_CHUNK == 0 and ts < SUBLANES

    pos = jnp.concatenate([jnp.tile(jnp.arange(tp, dtype=F32), bp), jnp.tile(PAST_LEN + jnp.arange(ts, dtype=F32), bs)])
    cos_t, sin_s = _rope_tables(pos)
    bd = jnp.asarray(np.kron(np.eye(LANES // ROPE_DIM), np.ones((ROPE_DIM, ROPE_DIM))), BF16)
    tril = np.tril(np.ones((SSM_CHUNK, SSM_CHUNK)))
    tril3 = jnp.asarray(np.concatenate([tril] * 3, axis=1), BF16)
    e1 = np.zeros((LANES, ssm_width))
    for hh in range(ssm_heads):
        e1[DT_LANE0 + hh, hh * SSM_HEAD_DIM:(hh + 1) * SSM_HEAD_DIM] = 1.0
    e3 = jnp.asarray(np.concatenate([e1] * 3, axis=0), BF16)

    h = jnp.concatenate([x_prompt.reshape(mp, d_model), x_sample.reshape(ms, d_model)], axis=0)
    outs = [[] for _ in range(9)]
    n_prompt_blocks = mp // tm
    eye_c = jnp.eye(tm // GM_CHUNK, dtype=F32)
    eye_s = jnp.eye(bs, dtype=F32)

    for i in range(depth):
        bf = lambda w: w.astype(BF16)
        row = lambda v: v.reshape(1, -1).astype(F32)

        h = _ffn(h, row(norm_ffn1[i]), bf(w_ffn1_gate[i]), bf(w_ffn1_up[i]), bf(w_ffn1_down[i]), tm=tm, tf=256)
        wi = w_in[i]
        w_in_p = jnp.concatenate(
            [wi[:, :dt0], wi[:, dt0 + ssm_heads:], wi[:, dt0:dt0 + ssm_heads],
             jnp.zeros((d_model, n_in - wi.shape[1]), wi.dtype)], axis=1)
        a = _norm_matmul(h, row(norm_mix[i]), bf(w_in_p), tm=tm, tn=n_in // 5)

        ws = gm_w_s[i]
        tril_c = jnp.tril(jnp.ones((GM_CHUNK, GM_CHUNK), F32))
        w_prompt = jnp.einsum('ab,gts->gatbs', eye_c, ws * tril_c).reshape(gm_heads, tm, tm)
        w_samp = jnp.einsum('ab,gts->gatbs', eye_s, ws[:, :ts, :ts] * tril_c[:ts, :ts]).reshape(gm_heads, tm, tm)
        wbig = bf(jnp.stack([w_prompt, w_samp]))
        bias_p = jnp.repeat(jnp.tile(gm_b_s[i].T, (tm // GM_CHUNK, 1)), GM_HEAD_DIM, axis=1)
        bias_s = jnp.repeat(jnp.tile(gm_b_s[i][:, :ts].T, (bs, 1)), GM_HEAD_DIM, axis=1)
        o_gm, vn = _sgu(a, row(gm_norm_v[i]), wbig, jnp.stack([bias_p, bias_s]).astype(F32),
                        rows=tm, n_prompt_blocks=n_prompt_blocks)

        ssd_params = (ssm_conv_w[i].astype(F32), row(ssm_conv_b[i]), _slab(ssm_dt_bias[i], DT_LANE0),
                      _slab(ssm_a_log[i], DT_LANE0), row(jnp.repeat(ssm_d[i], SSM_HEAD_DIM)), row(ssm_norm[i]),
                      tril3, e3)
        o_ssm_p, ssm_p, conv_p = _ssd_prompt(a, ssd_params, batch=bp, seq=tp,
                                             col_z=col_z, col_xbc=col_xbc, col_slab=col_slab)
        a_s = a[mp:]
        cs4 = jnp.pad(state_conv[i].astype(F32), ((0, 0), (0, ts - (SSM_CONV - 1)), (0, 0))).reshape(ms, conv_ch)
        o_ssm_s, ssm_s, conv_s = _ssd_sample(a_s, cs4, state_ssm[i].astype(F32), ssd_params, steps=ts, nreq_blk=16,
                                             col_z=col_z, col_xbc=col_xbc, col_slab=col_slab)

        wuq = mla_w_uq[i]
        wuq_p = jnp.concatenate([wuq[:, :, :NOPE_DIM].reshape(q_rank, -1), wuq[:, :, NOPE_DIM:].reshape(q_rank, -1)], axis=1)
        mla_params = (row(mla_q_norm[i]), bf(wuq_p), row(mla_qn_norm[i]), row(jnp.tile(mla_qr_norm[i], LANES // ROPE_DIM)),
                      row(mla_kv_norm[i]), _slab(mla_kr_norm[i], 0), bd)
        qn, qr, c_lat, k_rope, kr2 = _mla_proj(a, cos_t, sin_s, mla_params, tm=tm,
                                               col_cq=col_cq, col_ckv=col_ckv, col_slab=col_slab)
        wuk = mla_w_uk[i]
        wuv = mla_w_uv[i]
        gkn = row(mla_kn_norm[i])
        k_p, v_p = _kv_proj(c_lat, bf(wuk.reshape(kv_rank, -1)), bf(wuv.reshape(kv_rank, -1)), gkn, rows=mp, tk=tm)
        o_mla_p = _flash(qn, qr, k_p, kr2, v_p, batch=bp, seq=tp, blk=tm, scale=scale)

        w_absorb = bf((wuk * mla_kn_norm[i][None, None, :]).transpose(1, 2, 0))
        qp = _absorb(qn, w_absorb, row0=mp, rows=ms)
        pad_t = ((0, 0), (0, 0), (0, SUBLANES - ts), (0, 0))
        qp = jnp.pad(qp.reshape(mla_heads, bs, ts, kv_rank).transpose(1, 0, 2, 3), pad_t)
        qr_s = jnp.pad(qr[mp:].astype(F32).reshape(bs, ts, mla_heads, ROPE_DIM).transpose(0, 2, 1, 3), pad_t)
        wukt = bf(wuk.reshape(kv_rank, -1).T)
        o_lat = _paged(page_table, qp, qr_s, c_lat[mp:].reshape(bs, ts, kv_rank), k_rope[mp:].reshape(bs, ts, LANES),
                       wukt, cache_mla_latent, cache_mla_rope, layer=i, pages=8, scale=scale)
        o_lat = o_lat.reshape(bs, mla_heads, SUBLANES, kv_rank)[:, :, :ts].transpose(1, 0, 2, 3).reshape(mla_heads, ms, kv_rank)
        o_mla_s = _v_up(o_lat, bf(wuv.transpose(1, 0, 2)))

        mix = jnp.concatenate([o_gm,
                               jnp.concatenate([o_ssm_p, o_ssm_s], axis=0),
                               jnp.concatenate([o_mla_p, o_mla_s], axis=0)], axis=1)
        h = _out_proj(h, mix, bf(w_out[i]), tm=tm, tn=1024)
        h = _ffn(h, row(norm_ffn2[i]), bf(w_ffn2_gate[i]), bf(w_ffn2_up[i]), bf(w_ffn2_down[i]), tm=tm, tf=256)
        pe = jnp.concatenate([p_prompt[i].reshape(mp, -1), p_sample[i].reshape(ms, -1)], axis=0)
        h = _ple(h, row(norm_ple[i]), pe, bf(w_ple_gate[i]), bf(w_ple_proj[i]), tm=tm, tn=1024)

        new = (c_lat[:mp].reshape(bp, tp, kv_rank), k_rope[:mp, :ROPE_DIM].reshape(bp, tp, ROPE_DIM),
               ssm_p, conv_p[:, SUBLANES - (SSM_CONV - 1):],
               c_lat[mp:].reshape(bs, ts, kv_rank), k_rope[mp:, :ROPE_DIM].reshape(bs, ts, ROPE_DIM),
               ssm_s, conv_s.reshape(bs, ts, conv_ch)[:, :SSM_CONV - 1], vn[mp:].reshape(bs, ts, gm_width))
        for lst, val in zip(outs, new):
            lst.append(val)

    return (h[:mp].reshape(bp, tp, d_model), h[mp:].reshape(bs, ts, d_model)) + tuple(jnp.stack(o) for o in outs)
```

```python
import functools

import numpy as np
import jax
import jax.numpy as jnp
from jax import lax
from jax.experimental import pallas as pl
from jax.experimental.pallas import tpu as pltpu

F32 = jnp.float32
BF16 = jnp.bfloat16

EPS = 1e-6
ROPE_BASE = 10000.0
PAST_LEN = 8192
PAGE_SIZE = 128

LANES = 128
SUBLANES = 8
MIB = 1024 * 1024

GM_HEAD_DIM = 128
GM_CHUNK = 128
SSM_HEAD_DIM = 64
SSM_GROUPS = 2
SSM_STATE = 128
SSM_CONV = 4
SSM_CHUNK = 128
NOPE_DIM = 128
ROPE_DIM = 64
V_DIM = 128
Q_BLOCK = 128

DT_LANE0 = ROPE_DIM


def _cparams(sem, vmem_mib):
    return pltpu.CompilerParams(dimension_semantics=sem, vmem_limit_bytes=int(vmem_mib * MIB))


def _nt(a, b):
    return lax.dot_general(a, b, (((1,), (1,)), ((), ())), preferred_element_type=F32)


def _dot(a, b):
    return jnp.dot(a, b, preferred_element_type=F32)


def _silu(x):
    return x * jax.nn.sigmoid(x)


def _softplus(x):
    return jnp.maximum(x, 0.0) + jnp.log1p(jnp.exp(-jnp.abs(x)))


def _gelu_tanh(x):
    return 0.5 * x * (1.0 + jnp.tanh(np.sqrt(2.0 / np.pi).astype(np.float32) * (x + 0.044715 * (x * x * x))))


def _rms(x, gain):
    return x * lax.rsqrt(jnp.mean(x * x, axis=-1, keepdims=True) + EPS) * gain


def _split3(x):
    hi = x.astype(BF16)
    r1 = x - hi.astype(F32)
    mid = r1.astype(BF16)
    lo = (r1 - mid.astype(F32)).astype(BF16)
    return hi, mid, lo


def _expand_heads(x, e3):
    return _dot(jnp.concatenate(_split3(x), axis=1), e3)


def _ffn_body(x_ref, g_ref, wg_ref, wu_ref, wd_ref, o_ref, xn_ref):
    @pl.when(pl.program_id(1) == 0)
    def _():
        x = x_ref[...]
        xn_ref[...] = _rms(x, g_ref[...]).astype(BF16)
        o_ref[...] = x

    xn = xn_ref[...]
    gate = _dot(xn, wg_ref[...])
    up = _dot(xn, wu_ref[...])
    act = (_silu(gate) * up * 0.5).astype(BF16)
    dc = FFN_DOWN_CHUNK
    for n in range(o_ref.shape[1] // dc):
        o_ref[:, n * dc:(n + 1) * dc] += _dot(act, wd_ref[:, n * dc:(n + 1) * dc])


FFN_DOWN_CHUNK = 1024


def _ffn(h, g, wg, wu, wd, *, tm, tf):
    m, d = h.shape
    f = wg.shape[1]
    vmem = (tm * d * 4 * 4 + tm * d * 2 + 3 * 2 * d * tf * 2 + 4 * tm * tf * 4 + 2 * tm * FFN_DOWN_CHUNK * 4) / MIB + 4
    return pl.pallas_call(
        _ffn_body,
        grid=(m // tm, f // tf),
        in_specs=[
            pl.BlockSpec((tm, d), lambda i, j: (i, 0), pipeline_mode=pl.Buffered(1)),
            pl.BlockSpec((1, d), lambda i, j: (0, 0)),
            pl.BlockSpec((d, tf), lambda i, j: (0, j)),
            pl.BlockSpec((d, tf), lambda i, j: (0, j)),
            pl.BlockSpec((tf, d), lambda i, j: (j, 0)),
        ],
        out_specs=pl.BlockSpec((tm, d), lambda i, j: (i, 0)),
        out_shape=jax.ShapeDtypeStruct((m, d), F32),
        scratch_shapes=[pltpu.VMEM((tm, d), BF16)],
        compiler_params=_cparams(("parallel", "arbitrary"), vmem),
        name="ffn",
    )(h, g, wg, wu, wd)


def _norm_mm_body(x_ref, g_ref, w_ref, o_ref, xn_ref):
    @pl.when(pl.program_id(1) == 0)
    def _():
        xn_ref[...] = _rms(x_ref[...], g_ref[...]).astype(BF16)

    o_ref[...] = _dot(xn_ref[...], w_ref[...])


def _norm_matmul(h, g, w, *, tm, tn):
    m, d = h.shape
    n = w.shape[1]
    vmem = (2 * tm * d * 4 + tm * d * 2 + 2 * d * tn * 2 + 3 * tm * tn * 4) / MIB + 4
    return pl.pallas_call(
        _norm_mm_body,
        grid=(m // tm, n // tn),
        in_specs=[
            pl.BlockSpec((tm, d), lambda i, j: (i, 0), pipeline_mode=pl.Buffered(1)),
            pl.BlockSpec((1, d), lambda i, j: (0, 0)),
            pl.BlockSpec((d, tn), lambda i, j: (0, j)),
        ],
        out_specs=pl.BlockSpec((tm, tn), lambda i, j: (i, j)),
        out_shape=jax.ShapeDtypeStruct((m, n), F32),
        scratch_shapes=[pltpu.VMEM((tm, d), BF16)],
        compiler_params=_cparams(("parallel", "arbitrary"), vmem),
        name="in_proj",
    )(h, g, w)


def _out_proj_body(gm_ref, ssm_p_ref, ssm_s_ref, mla_p_ref, mla_s_ref, w_ref, res_ref, o_ref, *, n_prompt_blocks):
    is_sample = pl.program_id(0) >= n_prompt_blocks
    ssm = jnp.where(is_sample, ssm_s_ref[...], ssm_p_ref[...])
    mla = jnp.where(is_sample, mla_s_ref[...], mla_p_ref[...])
    k1 = gm_ref.shape[1]
    k2 = k1 + ssm.shape[1]
    o_ref[...] = (res_ref[...] + _dot(gm_ref[...], w_ref[0:k1, :]) + _dot(ssm, w_ref[k1:k2, :])
                  + _dot(mla, w_ref[k2:, :]))


def _out_proj(h, gm, ssm_p, ssm_s, mla_p, mla_s, w, *, tm, tn):
    m, d = h.shape
    k = w.shape[0]
    npb = ssm_p.shape[0] // tm
    assert ssm_s.shape[0] == tm and mla_s.shape[0] == tm
    vmem = (2 * 2 * tm * k * 2 + 2 * k * tn * 2 + 6 * tm * tn * 4) / MIB + 4
    prompt = lambda width: pl.BlockSpec((tm, width), lambda i, j: (jnp.minimum(i, npb - 1), 0))
    sample = lambda width: pl.BlockSpec((tm, width), lambda i, j: (0, 0))
    return pl.pallas_call(
        functools.partial(_out_proj_body, n_prompt_blocks=npb),
        grid=(m // tm, d // tn),
        in_specs=[
            pl.BlockSpec((tm, gm.shape[1]), lambda i, j: (i, 0)),
            prompt(ssm_p.shape[1]), sample(ssm_s.shape[1]), prompt(mla_p.shape[1]), sample(mla_s.shape[1]),
            pl.BlockSpec((k, tn), lambda i, j: (0, j)),
            pl.BlockSpec((tm, tn), lambda i, j: (i, j)),
        ],
        out_specs=pl.BlockSpec((tm, tn), lambda i, j: (i, j)),
        out_shape=jax.ShapeDtypeStruct((m, d), F32),
        compiler_params=_cparams(("parallel", "arbitrary"), vmem),
        name="out_proj",
    )(gm, ssm_p, ssm_s, mla_p, mla_s, w, h)


def _ple_body(x_ref, g_ref, pe_ref, wg_ref, wp_ref, res_ref, o_ref, xn_ref):
    @pl.when(pl.program_id(1) == 0)
    def _():
        xn_ref[...] = _rms(x_ref[...], g_ref[...]).astype(BF16)

    gate = jax.nn.sigmoid(_dot(xn_ref[...], wg_ref[...]))
    proj = _dot(pe_ref[...].astype(BF16), wp_ref[...])
    o_ref[...] = res_ref[...] + gate * proj


def _ple(h, g, pe, wg, wp, *, tm, tn):
    m, d = h.shape
    dp = pe.shape[1]
    vmem = (2 * tm * d * 4 + tm * d * 2 + 2 * tm * dp * 4 + 2 * d * tn * 2 + 2 * dp * tn * 2 + 8 * tm * tn * 4) / MIB + 4
    return pl.pallas_call(
        _ple_body,
        grid=(m // tm, d // tn),
        in_specs=[
            pl.BlockSpec((tm, d), lambda i, j: (i, 0), pipeline_mode=pl.Buffered(1)),
            pl.BlockSpec((1, d), lambda i, j: (0, 0)),
            pl.BlockSpec((tm, dp), lambda i, j: (i, 0)),
            pl.BlockSpec((d, tn), lambda i, j: (0, j)),
            pl.BlockSpec((dp, tn), lambda i, j: (0, j)),
            pl.BlockSpec((tm, tn), lambda i, j: (i, j)),
        ],
        out_specs=pl.BlockSpec((tm, tn), lambda i, j: (i, j)),
        out_shape=jax.ShapeDtypeStruct((m, d), F32),
        scratch_shapes=[pltpu.VMEM((tm, d), BF16)],
        compiler_params=_cparams(("parallel", "arbitrary"), vmem),
        name="ple",
    )(h, g, pe, wg, wp, h)


def _sgu_body(u_ref, v_ref, gain_ref, w_ref, bias_ref, o_ref, vn_ref, *, heads):
    u = _gelu_tanh(u_ref[...])
    v = _gelu_tanh(v_ref[...])
    for g in range(heads):
        sl = slice(g * GM_HEAD_DIM, (g + 1) * GM_HEAD_DIM)
        vn = _rms(v[:, sl], gain_ref[:, sl])
        vn_ref[:, sl] = vn
        mixed = _dot(w_ref[0, g], vn.astype(BF16)) + bias_ref[0, :, sl]
        o_ref[:, sl] = (u[:, sl] * mixed).astype(o_ref.dtype)


def _sgu(a, gain, wbig, bias, *, rows, n_prompt_blocks):
    m = a.shape[0]
    width = gain.shape[1]
    heads = width // GM_HEAD_DIM
    sel = lambda i: jnp.minimum(i // n_prompt_blocks, 1)
    return pl.pallas_call(
        functools.partial(_sgu_body, heads=heads),
        grid=(m // rows,),
        in_specs=[
            pl.BlockSpec((rows, width), lambda i: (i, 0)),
            pl.BlockSpec((rows, width), lambda i: (i, 1)),
            pl.BlockSpec((1, width), lambda i: (0, 0)),
            pl.BlockSpec((1, heads, rows, rows), lambda i: (sel(i), 0, 0, 0)),
            pl.BlockSpec((1, rows, width), lambda i: (sel(i), 0, 0)),
        ],
        out_specs=[
            pl.BlockSpec((rows, width), lambda i: (i, 0)),
            pl.BlockSpec((rows, width), lambda i: (0, 0)),
        ],
        out_shape=[jax.ShapeDtypeStruct((m, width), BF16), jax.ShapeDtypeStruct((rows, width), F32)],
        compiler_params=_cparams(("arbitrary",), 40),
        name="sgu",
    )(a, a, gain, wbig, bias)


def _ssd_common_tail(y, xs, z, dfull, norm, o_ref, width):
    y = y + xs * dfull
    gt = y * _silu(z)
    gw = width // SSM_GROUPS
    for g in range(SSM_GROUPS):
        sl = slice(g * gw, (g + 1) * gw)
        seg = gt[:, sl]
        o_ref[:, sl] = (seg * lax.rsqrt(jnp.mean(seg * seg, axis=-1, keepdims=True) + EPS) * norm[:, sl]).astype(o_ref.dtype)


def _dt_slab(slab_ref, dtb_ref, alog_ref, heads):
    lane = lax.broadcasted_iota(jnp.int32, slab_ref.shape, 1)
    raw = jnp.where((lane >= DT_LANE0) & (lane < DT_LANE0 + heads), slab_ref[...], 0.0)
    dt = _softplus(raw + dtb_ref[...])
    return dt, dt * (-jnp.exp(alog_ref[...]))


def _ssd_prompt_body(z_ref, xbc_ref, slab_ref, cw_ref, cb_ref, dtb_ref, alog_ref, dfull_ref, norm_ref, tril3_ref,
                     e3_ref, o_ref, st_out_ref, conv_out_ref, full_ref, state_ref, y_ref, *, chunk, n_chunks, heads):
    c = pl.program_id(1)
    width = heads * SSM_HEAD_DIM
    hpg = heads // SSM_GROUPS
    gw = hpg * SSM_HEAD_DIM
    tail = SUBLANES

    @pl.when(c == 0)
    def _():
        full_ref[0:tail, :] = jnp.zeros((tail, full_ref.shape[1]), F32)
        state_ref[...] = jnp.zeros(state_ref.shape, F32)

    full_ref[tail:tail + chunk, :] = xbc_ref[...]
    conv = cb_ref[...]
    for k in range(SSM_CONV):
        conv = conv + cw_ref[k:k + 1, :] * full_ref[pl.ds(tail - (SSM_CONV - 1) + k, chunk), :]
    full_ref[0:tail, :] = full_ref[chunk:chunk + tail, :]
    xc = _silu(conv)
    xs = xc[:, :width]
    bm = xc[:, width:width + SSM_GROUPS * SSM_STATE].astype(BF16)
    cm = xc[:, width + SSM_GROUPS * SSM_STATE:].astype(BF16)

    dt, a = _dt_slab(slab_ref, dtb_ref, alog_ref, heads)
    acs = _dot(tril3_ref[...], jnp.concatenate(_split3(a), axis=0))
    acs_t = acs.T
    arem = acs[chunk - 1:chunk, :] - acs
    e3 = e3_ref[...]
    xdt = xs * _expand_heads(dt, e3)
    ecs = jnp.exp(_expand_heads(acs, e3))
    xw_t = (xdt * jnp.exp(_expand_heads(arem, e3))).T.astype(BF16)
    ecol = ecs.T[:, chunk - 1:chunk]
    xdt_b = xdt.astype(BF16)

    row_i = lax.broadcasted_iota(jnp.int32, (chunk, chunk), 0)
    col_i = lax.broadcasted_iota(jnp.int32, (chunk, chunk), 1)
    causal = row_i >= col_i
    for g in range(SSM_GROUPS):
        cg = cm[:, g * SSM_STATE:(g + 1) * SSM_STATE]
        bg = bm[:, g * SSM_STATE:(g + 1) * SSM_STATE]
        cb = _nt(cg, bg)
        st = state_ref[g * hpg:(g + 1) * hpg].reshape(gw, SSM_STATE)
        y_ref[:, g * gw:(g + 1) * gw] = _nt(cg, st.astype(BF16)) * ecs[:, g * gw:(g + 1) * gw]
        for hh in range(hpg):
            h = g * hpg + hh
            col = acs[:, DT_LANE0 + h:DT_LANE0 + h + 1]
            row = acs_t[DT_LANE0 + h:DT_LANE0 + h + 1, :]
            mat = jnp.where(causal, cb * jnp.exp(col - row), 0.0).astype(BF16)
            hs = slice(h * SSM_HEAD_DIM, (h + 1) * SSM_HEAD_DIM)
            y_ref[:, hs] += _dot(mat, xdt_b[:, hs])
        s_new = _dot(xw_t[g * gw:(g + 1) * gw, :], bg)
        state_ref[g * hpg:(g + 1) * hpg] = (st * ecol[g * gw:(g + 1) * gw, :] + s_new).reshape(hpg, SSM_HEAD_DIM, SSM_STATE)

    _ssd_common_tail(y_ref[...], xs, z_ref[...], dfull_ref[...], norm_ref[...], o_ref, width)

    @pl.when(c == n_chunks - 1)
    def _():
        st_out_ref[0] = state_ref[...]
        conv_out_ref[0] = full_ref[0:tail, :]


def _ssd_prompt(a, params, *, batch, seq, col_z, col_xbc, col_slab):
    cw, cb, dtb, alog, dfull, norm, tril3, e3 = params
    width = norm.shape[1]
    heads = width // SSM_HEAD_DIM
    conv_ch = cw.shape[1]
    chunk = SSM_CHUNK
    nc = seq // chunk
    const = lambda shape: pl.BlockSpec(shape, lambda b, c: (0,) * len(shape))
    rowblk = lambda b, c: b * nc + c
    return pl.pallas_call(
        functools.partial(_ssd_prompt_body, chunk=chunk, n_chunks=nc, heads=heads),
        grid=(batch, nc),
        in_specs=[
            pl.BlockSpec((chunk, width), lambda b, c: (rowblk(b, c), col_z // width)),
            pl.BlockSpec((chunk, conv_ch), lambda b, c: (rowblk(b, c), col_xbc // conv_ch)),
            pl.BlockSpec((chunk, LANES), lambda b, c: (rowblk(b, c), col_slab // LANES)),
            const(cw.shape), const(cb.shape), const(dtb.shape), const(alog.shape), const(dfull.shape),
            const(norm.shape), const(tril3.shape), const(e3.shape),
        ],
        out_specs=[
            pl.BlockSpec((chunk, width), lambda b, c: (rowblk(b, c), 0)),
            pl.BlockSpec((1, heads, SSM_HEAD_DIM, SSM_STATE), lambda b, c: (b, 0, 0, 0)),
            pl.BlockSpec((1, SUBLANES, conv_ch), lambda b, c: (b, 0, 0)),
        ],
        out_shape=[
            jax.ShapeDtypeStruct((batch * seq, width), BF16),
            jax.ShapeDtypeStruct((batch, heads, SSM_HEAD_DIM, SSM_STATE), F32),
            jax.ShapeDtypeStruct((batch, SUBLANES, conv_ch), F32),
        ],
        scratch_shapes=[
            pltpu.VMEM((chunk + SUBLANES, conv_ch), F32),
            pltpu.VMEM((heads, SSM_HEAD_DIM, SSM_STATE), F32),
            pltpu.VMEM((chunk, width), F32),
        ],
        compiler_params=_cparams(("parallel", "arbitrary"), 40),
        name="ssd_prompt",
    )(a, a, a, cw, cb, dtb, alog, dfull, norm, tril3, e3)


def _ssd_sample_body(z_ref, xbc_ref, slab_ref, cs_ref, st_ref, cw_ref, cb_ref, dtb_ref, alog_ref, dfull_ref, norm_ref,
                     e3_ref, o_ref, st_out_ref, conv_out_ref, *, rows, steps, heads):
    width = heads * SSM_HEAD_DIM
    hpg = heads // SSM_GROUPS
    gw = hpg * SSM_HEAD_DIM
    nreq = rows // steps
    x = xbc_ref[...]
    cs = cs_ref[...]
    t_of = lambda shape: lax.broadcasted_iota(jnp.int32, shape, 0) % steps

    def back(v, j):
        return v if j == 0 else pltpu.roll(v, j, axis=0)

    def fwd(v, j):
        return v if j == 0 else pltpu.roll(v, rows - j, axis=0)

    tx = t_of(x.shape)
    conv = cb_ref[...] + cw_ref[SSM_CONV - 1:SSM_CONV, :] * x
    for k in range(SSM_CONV - 1):
        j = SSM_CONV - 1 - k
        conv = conv + cw_ref[k:k + 1, :] * jnp.where(tx >= j, back(x, j), fwd(cs, k))
    conv_out_ref[...] = fwd(x, steps - (SSM_CONV - 1))
    xc = _silu(conv)
    xs = xc[:, :width]
    bm = xc[:, width:width + SSM_GROUPS * SSM_STATE]
    cm = xc[:, width + SSM_GROUPS * SSM_STATE:]

    dt, a = _dt_slab(slab_ref, dtb_ref, alog_ref, heads)
    ts = t_of(a.shape)
    acs = a
    arem = jnp.zeros_like(a)
    for j in range(1, steps):
        acs = acs + jnp.where(ts >= j, back(a, j), 0.0)
        arem = arem + jnp.where(ts < steps - j, fwd(a, j), 0.0)
    e3 = e3_ref[...]
    acsx = _expand_heads(acs, e3)
    xdt = xs * _expand_heads(dt, e3)
    ecs = jnp.exp(acsx)
    xw = xdt * jnp.exp(_expand_heads(arem, e3))

    tw = t_of(xs.shape)
    y = jnp.zeros_like(xs)
    for j in range(steps):
        bj = back(bm, j)
        dec = jnp.exp(acsx - back(acsx, j)) * back(xdt, j)
        parts = []
        for g in range(SSM_GROUPS):
            sl = slice(g * SSM_STATE, (g + 1) * SSM_STATE)
            cbj = jnp.sum(cm[:, sl] * bj[:, sl], axis=-1, keepdims=True)
            parts.append(cbj * dec[:, g * gw:(g + 1) * gw])
        y = y + jnp.where(tw >= j, jnp.concatenate(parts, axis=1), 0.0)

    pad = jnp.zeros((LANES - rows, width), F32)
    xw_t = jnp.concatenate([xw, pad], axis=0).T.astype(BF16)
    e_t = jnp.concatenate([ecs, pad], axis=0).T
    cm_b = cm.astype(BF16)
    bm_pad = jnp.concatenate([bm, jnp.zeros((LANES - rows, bm.shape[1]), F32)], axis=0)
    req_y = lax.broadcasted_iota(jnp.int32, (rows, gw), 0) // steps
    req_b = lax.broadcasted_iota(jnp.int32, (LANES, SSM_STATE), 0) // steps
    yoff = [jnp.zeros((rows, gw), F32) for _ in range(SSM_GROUPS)]
    for b in range(nreq):
        last = b * steps + steps - 1
        for g in range(SSM_GROUPS):
            sl = slice(g * SSM_STATE, (g + 1) * SSM_STATE)
            h0 = st_ref[0, b, g * hpg:(g + 1) * hpg].reshape(gw, SSM_STATE)
            yoff[g] = yoff[g] + jnp.where(req_y == b, _nt(cm_b[:, sl], h0.astype(BF16)), 0.0)
            b_only = jnp.where(req_b == b, bm_pad[:, sl], 0.0).astype(BF16)
            s_new = _dot(xw_t[g * gw:(g + 1) * gw, :], b_only)
            decay = e_t[g * gw:(g + 1) * gw, last:last + 1]
            st_out_ref[b, g * hpg:(g + 1) * hpg] = (h0 * decay + s_new).reshape(hpg, SSM_HEAD_DIM, SSM_STATE)
    y = y + jnp.concatenate(yoff, axis=1) * ecs

    _ssd_common_tail(y, xs, z_ref[...], dfull_ref[...], norm_ref[...], o_ref, width)


def _ssd_sample(a_s, cs4, state, params, *, layer, steps, nreq_blk, col_z, col_xbc, col_slab):
    cw, cb, dtb, alog, dfull, norm, _, e3 = params
    width = norm.shape[1]
    heads = width // SSM_HEAD_DIM
    conv_ch = cw.shape[1]
    m = a_s.shape[0]
    rows = nreq_blk * steps
    const = lambda shape: pl.BlockSpec(shape, lambda i: (0,) * len(shape))
    st_spec = pl.BlockSpec((nreq_blk, heads, SSM_HEAD_DIM, SSM_STATE), lambda i: (i, 0, 0, 0))
    st_in_spec = pl.BlockSpec((1, nreq_blk, heads, SSM_HEAD_DIM, SSM_STATE), lambda i: (layer, i, 0, 0, 0))
    return pl.pallas_call(
        functools.partial(_ssd_sample_body, rows=rows, steps=steps, heads=heads),
        grid=(m // rows,),
        in_specs=[
            pl.BlockSpec((rows, width), lambda i: (i, col_z // width)),
            pl.BlockSpec((rows, conv_ch), lambda i: (i, col_xbc // conv_ch)),
            pl.BlockSpec((rows, LANES), lambda i: (i, col_slab // LANES)),
            pl.BlockSpec((rows, conv_ch), lambda i: (i, 0)),
            st_in_spec,
            const(cw.shape), const(cb.shape), const(dtb.shape), const(alog.shape), const(dfull.shape),
            const(norm.shape), const(e3.shape),
        ],
        out_specs=[
            pl.BlockSpec((rows, width), lambda i: (i, 0)),
            st_spec,
            pl.BlockSpec((rows, conv_ch), lambda i: (i, 0)),
        ],
        out_shape=[
            jax.ShapeDtypeStruct((m, width), BF16),
            jax.ShapeDtypeStruct(state.shape[1:], F32),
            jax.ShapeDtypeStruct((m, conv_ch), F32),
        ],
        compiler_params=_cparams(("parallel",), 48),
        name="ssd_sample",
    )(a_s, a_s, a_s, cs4, state, cw, cb, dtb, alog, dfull, norm, e3)


def _rms_pairs(x, bd):
    s = x * x
    hi = s.astype(BF16)
    lo = (s - hi.astype(F32)).astype(BF16)
    ssq = _dot(hi, bd) + _dot(lo, bd)
    return x * lax.rsqrt(ssq * (1.0 / ROPE_DIM) + EPS)


def _rope_pairs(y, cos_t, sin_s):
    lane = lax.broadcasted_iota(jnp.int32, y.shape, 1)
    half = ROPE_DIM // 2
    rot = jnp.where((lane % ROPE_DIM) < half, pltpu.roll(y, LANES - half, axis=1), pltpu.roll(y, half, axis=1))
    return y * cos_t + rot * sin_s


def _mla_proj_body(cq_ref, ckv_ref, slab_ref, cos_ref, sin_ref, gq_ref, wuq_ref, gqn_ref, gqr_ref, gkv_ref, gkr_ref,
                   bd_ref, qn_ref, qr_ref, c_ref, kr_ref, kr2_ref, *, heads, qscale):
    cqn = _rms(cq_ref[...], gq_ref[...]).astype(BF16)
    q = _dot(cqn, wuq_ref[...])
    gqn = gqn_ref[...]
    for h in range(heads):
        sl = slice(h * NOPE_DIM, (h + 1) * NOPE_DIM)
        qn_ref[:, sl] = (_rms(q[:, sl], gqn) * qscale).astype(BF16)
    bd = bd_ref[...]
    cos_t = cos_ref[...]
    sin_s = sin_ref[...]
    base = heads * NOPE_DIM
    for j in range(heads * ROPE_DIM // LANES):
        x = q[:, base + j * LANES: base + (j + 1) * LANES]
        y = _rope_pairs(_rms_pairs(x, bd) * gqr_ref[...], cos_t, sin_s)
        qr_ref[:, j * LANES:(j + 1) * LANES] = (y * qscale).astype(BF16)
    c_ref[...] = _rms(ckv_ref[...], gkv_ref[...])
    kr = _rope_pairs(_rms_pairs(slab_ref[...], bd) * gkr_ref[...], cos_t, sin_s)
    kr_ref[...] = kr
    kr2_ref[...] = (kr + pltpu.roll(kr, ROPE_DIM, axis=1)).astype(BF16)


def _mla_proj(a, cos_t, sin_s, params, *, tm, col_cq, col_ckv, col_slab, qscale):
    gq, wuq, gqn, gqr, gkv, gkr, bd = params
    m = a.shape[0]
    q_rank = gq.shape[1]
    kv_rank = gkv.shape[1]
    heads = wuq.shape[1] // (NOPE_DIM + ROPE_DIM)
    const = lambda shape: pl.BlockSpec(shape, lambda i: (0,) * len(shape))
    row = lambda w: pl.BlockSpec((tm, w), lambda i: (i, 0))
    return pl.pallas_call(
        functools.partial(_mla_proj_body, heads=heads, qscale=qscale),
        grid=(m // tm,),
        in_specs=[
            pl.BlockSpec((tm, q_rank), lambda i: (i, col_cq // q_rank)),
            pl.BlockSpec((tm, kv_rank), lambda i: (i, col_ckv // kv_rank)),
            pl.BlockSpec((tm, LANES), lambda i: (i, col_slab // LANES)),
            row(LANES), row(LANES),
            const(gq.shape), const(wuq.shape), const(gqn.shape), const(gqr.shape), const(gkv.shape),
            const(gkr.shape), const(bd.shape),
        ],
        out_specs=[row(heads * NOPE_DIM), row(heads * ROPE_DIM), row(kv_rank), row(LANES), row(LANES)],
        out_shape=[
            jax.ShapeDtypeStruct((m, heads * NOPE_DIM), BF16),
            jax.ShapeDtypeStruct((m, heads * ROPE_DIM), BF16),
            jax.ShapeDtypeStruct((m, kv_rank), F32),
            jax.ShapeDtypeStruct((m, LANES), F32),
            jax.ShapeDtypeStruct((m, LANES), BF16),
        ],
        compiler_params=_cparams(("parallel",), 48),
        name="mla_proj",
    )(a, a, a, cos_t, sin_s, gq, wuq, gqn, gqr, gkv, gkr, bd)


def _kv_proj_body(c_ref, kr2_ref, wuk_ref, wuv_ref, gkn_ref, k_ref, v_ref, *, heads):
    cb = c_ref[...].astype(BF16)
    k = _dot(cb, wuk_ref[...])
    gkn = gkn_ref[...]
    kr2 = kr2_ref[...]
    for h in range(heads):
        sl = slice(h * NOPE_DIM, (h + 1) * NOPE_DIM)
        k_ref[:, 2 * h * NOPE_DIM:(2 * h + 1) * NOPE_DIM] = _rms(k[:, sl], gkn).astype(BF16)
        k_ref[:, (2 * h + 1) * NOPE_DIM:(2 * h + 2) * NOPE_DIM] = kr2
    v_ref[...] = _dot(cb, wuv_ref[...]).astype(BF16)


def _kv_proj(c, kr2, wuk, wuv, gkn, *, rows, tk):
    kv_rank = c.shape[1]
    n = wuk.shape[1]
    heads = n // NOPE_DIM
    const = lambda shape: pl.BlockSpec(shape, lambda i: (0,) * len(shape))
    return pl.pallas_call(
        functools.partial(_kv_proj_body, heads=heads),
        grid=(rows // tk,),
        in_specs=[pl.BlockSpec((tk, kv_rank), lambda i: (i, 0)), pl.BlockSpec((tk, LANES), lambda i: (i, 0)),
                  const(wuk.shape), const(wuv.shape), const(gkn.shape)],
        out_specs=[pl.BlockSpec((tk, 2 * n), lambda i: (i, 0)), pl.BlockSpec((tk, n), lambda i: (i, 0))],
        out_shape=[jax.ShapeDtypeStruct((rows, 2 * n), BF16), jax.ShapeDtypeStruct((rows, n), BF16)],
        compiler_params=_cparams(("parallel",), 32),
        name="kv_proj",
    )(c, kr2, wuk, wuv, gkn)


FLASH_HEADS = 2


def _flash_body(qn_ref, qr_ref, k_ref, v_ref, o_ref, m_ref, l_ref, acc_ref, *, blk):
    i = pl.program_id(2)
    lane = lax.broadcasted_iota(jnp.int32, qr_ref.shape, 1)
    zero = jnp.zeros_like(qr_ref[...])
    q = []
    for hh in range(FLASH_HEADS):
        qr = jnp.where((lane // ROPE_DIM) == hh, qr_ref[...], zero)
        q.append(jnp.concatenate([qn_ref[:, hh * NOPE_DIM:(hh + 1) * NOPE_DIM], qr], axis=1))
    m_ref[...] = jnp.full(m_ref.shape, -jnp.inf, F32)
    l_ref[...] = jnp.zeros(l_ref.shape, F32)
    acc_ref[...] = jnp.zeros(acc_ref.shape, F32)
    kw = NOPE_DIM + LANES
    reps = blk // LANES

    def step(j, masked):
        rows = pl.ds(pl.multiple_of(j * blk, blk), blk)
        for hh in range(FLASH_HEADS):
            s = _nt(q[hh], k_ref[rows, hh * kw:(hh + 1) * kw])
            if masked:
                r = lax.broadcasted_iota(jnp.int32, s.shape, 0)
                c = lax.broadcasted_iota(jnp.int32, s.shape, 1)
                s = jnp.where(c <= r, s, -jnp.inf)
            m_old = m_ref[hh]
            m_new = jnp.maximum(m_old, jnp.max(s, axis=-1, keepdims=True))
            corr = jnp.exp2(m_old - m_new)
            p = jnp.exp2(s - jnp.tile(m_new, (1, reps)))
            l_ref[hh] = l_ref[hh] * corr + jnp.sum(p, axis=-1, keepdims=True)
            acc_ref[hh] = acc_ref[hh] * corr + _dot(p.astype(BF16), v_ref[rows, hh * V_DIM:(hh + 1) * V_DIM])
            m_ref[hh] = m_new

    def body(j, carry):
        step(j, False)
        return carry

    lax.fori_loop(0, i, body, 0)
    step(i, True)
    for hh in range(FLASH_HEADS):
        o_ref[:, hh * V_DIM:(hh + 1) * V_DIM] = (acc_ref[hh] / l_ref[hh]).astype(o_ref.dtype)


def _flash(qn, qr, kcat, v, *, batch, seq, blk):
    heads = v.shape[1] // V_DIM
    nb = seq // blk
    hp = FLASH_HEADS
    return pl.pallas_call(
        functools.partial(_flash_body, blk=blk),
        grid=(batch, heads // hp, nb),
        in_specs=[
            pl.BlockSpec((blk, hp * NOPE_DIM), lambda b, h, i: (b * nb + i, h)),
            pl.BlockSpec((blk, LANES), lambda b, h, i: (b * nb + i, h)),
            pl.BlockSpec((seq, hp * (NOPE_DIM + LANES)), lambda b, h, i: (b, h)),
            pl.BlockSpec((seq, hp * V_DIM), lambda b, h, i: (b, h)),
        ],
        out_specs=pl.BlockSpec((blk, hp * V_DIM), lambda b, h, i: (b * nb + i, h)),
        out_shape=jax.ShapeDtypeStruct((batch * seq, heads * V_DIM), BF16),
        scratch_shapes=[pltpu.VMEM((hp, blk, LANES), F32), pltpu.VMEM((hp, blk, LANES), F32),
                        pltpu.VMEM((hp, blk, V_DIM), F32)],
        compiler_params=_cparams(("parallel", "parallel", "arbitrary"), 32),
        name="flash",
    )(qn, qr, kcat, v)


def _absorb_body(qn_ref, w_ref, o_ref):
    o_ref[0] = _dot(qn_ref[...], w_ref[0]).astype(o_ref.dtype)


def _absorb(qn, w, *, row0, rows):
    heads, _, kv_rank = w.shape
    return pl.pallas_call(
        _absorb_body,
        grid=(heads,),
        in_specs=[
            pl.BlockSpec((rows, NOPE_DIM), lambda h: (row0 // rows, h)),
            pl.BlockSpec((1, NOPE_DIM, kv_rank), lambda h: (h, 0, 0)),
        ],
        out_specs=pl.BlockSpec((1, rows, kv_rank), lambda h: (h, 0, 0)),
        out_shape=jax.ShapeDtypeStruct((heads, rows, kv_rank), F32),
        compiler_params=_cparams(("parallel",), 16),
        name="absorb",
    )(qn, w)


def _v_up_body(o_ref_in, w_ref, o_ref):
    o_ref[...] = _dot(o_ref_in[0].astype(BF16), w_ref[0]).astype(o_ref.dtype)


def _v_up(o_lat, w):
    heads, rows, kv_rank = o_lat.shape
    return pl.pallas_call(
        _v_up_body,
        grid=(heads,),
        in_specs=[
            pl.BlockSpec((1, rows, kv_rank), lambda h: (h, 0, 0)),
            pl.BlockSpec((1, kv_rank, V_DIM), lambda h: (h, 0, 0)),
        ],
        out_specs=pl.BlockSpec((rows, V_DIM), lambda h: (0, h)),
        out_shape=jax.ShapeDtypeStruct((rows, heads * V_DIM), BF16),
        compiler_params=_cparams(("parallel",), 16),
        name="v_up",
    )(o_lat, w)


KEY_CHUNK = 512


def _paged_body(pt_ref, qp_ref, qr_ref, cnew_ref, krnew_ref, wuk_ref, ones_ref, *rest, pages, heads, steps):
    c_refs = rest[:pages]
    krt_refs = rest[pages:2 * pages]
    o_ref, m_ref, l_ref, acc_ref, cpad_ref, krpad_ref = rest[2 * pages:]
    j = pl.program_id(1)
    nj = pl.num_programs(1)
    qrows = heads * SUBLANES
    qp = qp_ref[0].reshape(qrows, qp_ref.shape[-1]).astype(BF16)
    qr = qr_ref[0].reshape(qrows, qr_ref.shape[-1]).astype(BF16)

    @pl.when(j == 0)
    def _():
        m_ref[...] = jnp.full(m_ref.shape, -jnp.inf, F32)
        l_ref[...] = jnp.zeros(l_ref.shape, F32)
        acc_ref[...] = jnp.zeros(acc_ref.shape, F32)

    def attend(c_blk, krt_blk, valid):
        folded = None
        for t in range(wuk_ref.shape[1] // KEY_CHUNK):
            k = _dot(c_blk, wuk_ref[:, t * KEY_CHUNK:(t + 1) * KEY_CHUNK])
            sq = k * k
            for u in range(KEY_CHUNK // LANES):
                tile = sq[:, u * LANES:(u + 1) * LANES]
                folded = tile if folded is None else folded + tile
        hi = folded.astype(BF16)
        lo = (folded - hi.astype(F32)).astype(BF16)
        ssq = _dot(hi, ones_ref[...]) + _dot(lo, ones_ref[...])
        rinv_t = lax.rsqrt(ssq * (1.0 / NOPE_DIM) + EPS).T
        s = _nt(qp, c_blk) * rinv_t + _dot(qr, krt_blk)
        if valid is not None:
            s = jnp.where(valid, s, -jnp.inf)
        m_old = m_ref[...]
        m_new = jnp.maximum(m_old, jnp.max(s, axis=-1, keepdims=True))
        corr = jnp.exp2(m_old - m_new)
        p = jnp.exp2(s - m_new)
        l_ref[...] = l_ref[...] * corr + jnp.sum(p, axis=-1, keepdims=True)
        acc_ref[...] = acc_ref[...] * corr + _dot(p.astype(BF16), c_blk)
        m_ref[...] = m_new

    half = pages // 2
    for g in range(2):
        ks = range(g * half, (g + 1) * half)
        c_blk = jnp.concatenate([c_refs[k][0, 0] for k in ks], axis=0).astype(BF16)
        krt_blk = jnp.concatenate([krt_refs[k][0, 0] for k in ks], axis=1).astype(BF16)
        attend(c_blk, krt_blk, None)

    @pl.when(j == nj - 1)
    def _():
        n = cpad_ref.shape[0]
        cpad_ref[...] = jnp.zeros(cpad_ref.shape, F32)
        krpad_ref[...] = jnp.zeros(krpad_ref.shape, F32)
        cpad_ref[0:steps, :] = cnew_ref[0]
        krpad_ref[0:steps, :] = krnew_ref[0]
        t = lax.broadcasted_iota(jnp.int32, (qrows, n), 0) % SUBLANES
        p = lax.broadcasted_iota(jnp.int32, (qrows, n), 1)
        krt_new = krpad_ref[...].T[:ROPE_DIM, :].astype(BF16)
        attend(cpad_ref[...].astype(BF16), krt_new, (p < steps) & (p <= t))
        o_ref[0] = acc_ref[...] / l_ref[...]


def _paged(page_table, qp, qr, c_new, kr_new, wuk, ones8, pool_c, pool_rt, *, layer, pages):
    nreq, heads, _, kv_rank = qp.shape
    steps = c_new.shape[1]
    n_pages = page_table.shape[1]
    qrows = heads * SUBLANES
    flat_pt = page_table.reshape(-1)

    def page_spec(shape, k):
        return pl.BlockSpec((1, 1) + shape, lambda b, j, pt: (layer, pt[b * n_pages + j * pages + k], 0, 0))

    const = lambda shape: pl.BlockSpec(shape, lambda b, j, pt: (0,) * len(shape))
    grid_spec = pltpu.PrefetchScalarGridSpec(
        num_scalar_prefetch=1,
        grid=(nreq, n_pages // pages),
        in_specs=[
            pl.BlockSpec((1, heads, SUBLANES, kv_rank), lambda b, j, pt: (b, 0, 0, 0)),
            pl.BlockSpec((1, heads, SUBLANES, ROPE_DIM), lambda b, j, pt: (b, 0, 0, 0)),
            pl.BlockSpec((1, steps, kv_rank), lambda b, j, pt: (b, 0, 0)),
            pl.BlockSpec((1, steps, LANES), lambda b, j, pt: (b, 0, 0)),
            const(wuk.shape), const(ones8.shape),
        ] + [page_spec((PAGE_SIZE, kv_rank), k) for k in range(pages)]
          + [page_spec((ROPE_DIM, PAGE_SIZE), k) for k in range(pages)],
        out_specs=pl.BlockSpec((1, qrows, kv_rank), lambda b, j, pt: (b, 0, 0)),
        scratch_shapes=[
            pltpu.VMEM((qrows, 1), F32), pltpu.VMEM((qrows, 1), F32), pltpu.VMEM((qrows, kv_rank), F32),
            pltpu.VMEM((LANES, kv_rank), F32), pltpu.VMEM((LANES, LANES), F32),
        ],
    )
    return pl.pallas_call(
        functools.partial(_paged_body, pages=pages, heads=heads, steps=steps),
        grid_spec=grid_spec,
        out_shape=jax.ShapeDtypeStruct((nreq, qrows, kv_rank), F32),
        compiler_params=_cparams(("parallel", "arbitrary"), 48),
        name="paged",
    )(flat_pt, qp, qr, c_new, kr_new, wuk, ones8, *([pool_c] * pages), *([pool_rt] * pages))


def _rope_tables(positions):
    half = ROPE_DIM // 2
    inv = ROPE_BASE ** (-jnp.arange(half, dtype=F32) / half)
    ang = positions[:, None] * inv[None, :]
    cos = jnp.cos(ang).astype(F32)
    sin = jnp.sin(ang).astype(F32)
    reps = LANES // ROPE_DIM
    return jnp.tile(cos, (1, 2 * reps)), jnp.tile(jnp.concatenate([-sin, sin], axis=1), (1, reps))


def _slab(vec, lane0):
    return jnp.zeros((1, LANES), F32).at[0, lane0:lane0 + vec.shape[0]].set(vec.astype(F32))


def kernel(x_prompt, x_sample, cache_mla_latent, cache_mla_rope, state_ssm, state_conv, page_table, p_prompt, p_sample, norm_ffn1, w_ffn1_gate, w_ffn1_up, w_ffn1_down, norm_mix, w_in, gm_norm_v, gm_w_s, gm_b_s, ssm_conv_w, ssm_conv_b, ssm_dt_bias, ssm_a_log, ssm_d, ssm_norm, mla_q_norm, mla_w_uq, mla_qn_norm, mla_qr_norm, mla_kv_norm, mla_kr_norm, mla_w_uk, mla_kn_norm, mla_w_uv, w_out, norm_ffn2, w_ffn2_gate, w_ffn2_up, w_ffn2_down, norm_ple, w_ple_gate, w_ple_proj):
    bp, tp, d_model = x_prompt.shape
    bs, ts, _ = x_sample.shape
    depth = norm_ffn1.shape[0]
    mp, ms = bp * tp, bs * ts
    gm_width = gm_norm_v.shape[1]
    gm_heads = gm_width // GM_HEAD_DIM
    ssm_width = ssm_norm.shape[1]
    ssm_heads = ssm_width // SSM_HEAD_DIM
    conv_ch = ssm_conv_w.shape[2]
    q_rank = mla_q_norm.shape[1]
    kv_rank = mla_kv_norm.shape[1]
    mla_heads = mla_w_uq.shape[2]
    qscale = float((NOPE_DIM + ROPE_DIM) ** -0.5 * np.log2(np.e))

    tm = ms
    col_v = gm_width
    col_z = 2 * gm_width
    col_xbc = col_z + ssm_width
    col_cq = col_xbc + conv_ch
    col_ckv = col_cq + q_rank
    col_slab = col_ckv + kv_rank
    n_in = col_slab + LANES
    dt0 = col_xbc + conv_ch
    assert col_v == gm_width and col_z % ssm_width == 0 and col_xbc % conv_ch == 0
    assert col_cq % q_rank == 0 and col_ckv % kv_rank == 0 and col_slab % LANES == 0
    assert mp % tm == 0 and ms == tm and tp % SSM_CHUNK == 0 and ts < SUBLANES

    pos = jnp.concatenate([jnp.tile(jnp.arange(tp, dtype=F32), bp), jnp.tile(PAST_LEN + jnp.arange(ts, dtype=F32), bs)])
    cos_t, sin_s = _rope_tables(pos)
    bd = jnp.asarray(np.kron(np.eye(LANES // ROPE_DIM), np.ones((ROPE_DIM, ROPE_DIM))), BF16)
    tril = np.tril(np.ones((SSM_CHUNK, SSM_CHUNK)))
    tril3 = jnp.asarray(np.concatenate([tril] * 3, axis=1), BF16)
    e1 = np.zeros((LANES, ssm_width))
    for hh in range(ssm_heads):
        e1[DT_LANE0 + hh, hh * SSM_HEAD_DIM:(hh + 1) * SSM_HEAD_DIM] = 1.0
    e3 = jnp.asarray(np.concatenate([e1] * 3, axis=0), BF16)
    assert mla_heads * SUBLANES == LANES
    ones8 = jnp.asarray(np.kron(np.ones((LANES // mla_heads, 1)), np.kron(np.eye(mla_heads), np.ones((1, SUBLANES)))), BF16)
    pool_rt = jnp.swapaxes(cache_mla_rope, 2, 3)

    h = jnp.concatenate([x_prompt.reshape(mp, d_model), x_sample.reshape(ms, d_model)], axis=0)
    outs = [[] for _ in range(9)]
    n_prompt_blocks = mp // tm
    eye_c = jnp.eye(tm // GM_CHUNK, dtype=F32)
    eye_s = jnp.eye(bs, dtype=F32)

    for i in range(depth):
        bf = lambda w: w.astype(BF16)
        row = lambda v: v.reshape(1, -1).astype(F32)

        h = _ffn(h, row(norm_ffn1[i]), bf(w_ffn1_gate[i]), bf(w_ffn1_up[i]), bf(w_ffn1_down[i]), tm=tm, tf=256)
        wi = w_in[i]
        w_in_p = jnp.concatenate(
            [wi[:, :dt0], wi[:, dt0 + ssm_heads:], wi[:, dt0:dt0 + ssm_heads],
             jnp.zeros((d_model, n_in - wi.shape[1]), wi.dtype)], axis=1)
        a = _norm_matmul(h, row(norm_mix[i]), bf(w_in_p), tm=tm, tn=n_in // 5)

        ws = gm_w_s[i]
        tril_c = jnp.tril(jnp.ones((GM_CHUNK, GM_CHUNK), F32))
        w_prompt = jnp.einsum('ab,gts->gatbs', eye_c, ws * tril_c).reshape(gm_heads, tm, tm)
        w_samp = jnp.einsum('ab,gts->gatbs', eye_s, ws[:, :ts, :ts] * tril_c[:ts, :ts]).reshape(gm_heads, tm, tm)
        wbig = bf(jnp.stack([w_prompt, w_samp]))
        bias_p = jnp.repeat(jnp.tile(gm_b_s[i].T, (tm // GM_CHUNK, 1)), GM_HEAD_DIM, axis=1)
        bias_s = jnp.repeat(jnp.tile(gm_b_s[i][:, :ts].T, (bs, 1)), GM_HEAD_DIM, axis=1)
        o_gm, vn = _sgu(a, row(gm_norm_v[i]), wbig, jnp.stack([bias_p, bias_s]).astype(F32),
                        rows=tm, n_prompt_blocks=n_prompt_blocks)

        ssd_params = (ssm_conv_w[i].astype(F32), row(ssm_conv_b[i]), _slab(ssm_dt_bias[i], DT_LANE0),
                      _slab(ssm_a_log[i], DT_LANE0), row(jnp.repeat(ssm_d[i], SSM_HEAD_DIM)), row(ssm_norm[i]),
                      tril3, e3)
        o_ssm_p, ssm_p, conv_p = _ssd_prompt(a, ssd_params, batch=bp, seq=tp,
                                             col_z=col_z, col_xbc=col_xbc, col_slab=col_slab)
        a_s = a[mp:]
        cs4 = jnp.pad(state_conv[i].astype(F32), ((0, 0), (0, ts - (SSM_CONV - 1)), (0, 0))).reshape(ms, conv_ch)
        o_ssm_s, ssm_s, conv_s = _ssd_sample(a_s, cs4, state_ssm, ssd_params, layer=i, steps=ts, nreq_blk=16,
                                             col_z=col_z, col_xbc=col_xbc, col_slab=col_slab)

        wuq = mla_w_uq[i]
        wuq_p = jnp.concatenate([wuq[:, :, :NOPE_DIM].reshape(q_rank, -1), wuq[:, :, NOPE_DIM:].reshape(q_rank, -1)], axis=1)
        mla_params = (row(mla_q_norm[i]), bf(wuq_p), row(mla_qn_norm[i]), row(jnp.tile(mla_qr_norm[i], LANES // ROPE_DIM)),
                      row(mla_kv_norm[i]), _slab(mla_kr_norm[i], 0), bd)
        qn, qr, c_lat, k_rope, kr2 = _mla_proj(a, cos_t, sin_s, mla_params, tm=tm,
                                               col_cq=col_cq, col_ckv=col_ckv, col_slab=col_slab, qscale=qscale)
        wuk = mla_w_uk[i]
        wuv = mla_w_uv[i]
        gkn = row(mla_kn_norm[i])
        wuk2 = bf(wuk.reshape(kv_rank, -1))
        k_p, v_p = _kv_proj(c_lat, kr2, wuk2, bf(wuv.reshape(kv_rank, -1)), gkn, rows=mp, tk=tm)
        o_mla_p = _flash(qn, qr, k_p, v_p, batch=bp, seq=tp, blk=tm)

        w_absorb = bf((wuk * mla_kn_norm[i][None, None, :]).transpose(1, 2, 0))
        qp = _absorb(qn, w_absorb, row0=mp, rows=ms)
        pad_t = ((0, 0), (0, 0), (0, SUBLANES - ts), (0, 0))
        qp = jnp.pad(qp.reshape(mla_heads, bs, ts, kv_rank).transpose(1, 0, 2, 3), pad_t)
        qr_s = jnp.pad(qr[mp:].astype(F32).reshape(bs, ts, mla_heads, ROPE_DIM).transpose(0, 2, 1, 3), pad_t)
        o_lat = _paged(page_table, qp, qr_s, c_lat[mp:].reshape(bs, ts, kv_rank), k_rope[mp:].reshape(bs, ts, LANES),
                       bf(wuk.transpose(0, 2, 1).reshape(kv_rank, -1)), ones8, cache_mla_latent, pool_rt,
                       layer=i, pages=16)
        o_lat = o_lat.reshape(bs, mla_heads, SUBLANES, kv_rank)[:, :, :ts].transpose(1, 0, 2, 3).reshape(mla_heads, ms, kv_rank)
        o_mla_s = _v_up(o_lat, bf(wuv.transpose(1, 0, 2)))

        h = _out_proj(h, o_gm, o_ssm_p, o_ssm_s, o_mla_p, o_mla_s, bf(w_out[i]), tm=tm, tn=1024)
        h = _ffn(h, row(norm_ffn2[i]), bf(w_ffn2_gate[i]), bf(w_ffn2_up[i]), bf(w_ffn2_down[i]), tm=tm, tf=256)
        pe = jnp.concatenate([p_prompt[i].reshape(mp, -1), p_sample[i].reshape(ms, -1)], axis=0)
        h = _ple(h, row(norm_ple[i]), pe, bf(w_ple_gate[i]), bf(w_ple_proj[i]), tm=tm, tn=1024)

        new = (c_lat[:mp].reshape(bp, tp, kv_rank), k_rope[:mp, :ROPE_DIM].reshape(bp, tp, ROPE_DIM),
               ssm_p, conv_p[:, SUBLANES - (SSM_CONV - 1):],
               c_lat[mp:].reshape(bs, ts, kv_rank), k_rope[mp:, :ROPE_DIM].reshape(bs, ts, ROPE_DIM),
               ssm_s, conv_s.reshape(bs, ts, conv_ch)[:, :SSM_CONV - 1], vn.reshape(bs, ts, gm_width))
        for lst, val in zip(outs, new):
            lst.append(val)

    return (h[:mp].reshape(bp, tp, d_model), h[mp:].reshape(bs, ts, d_model)) + tuple(jnp.stack(o) for o in outs)
```

```python
import functools

import numpy as np
import jax
import jax.numpy as jnp
from jax import lax
from jax.experimental import pallas as pl
from jax.experimental.pallas import tpu as pltpu

F32 = jnp.float32
BF16 = jnp.bfloat16

EPS = 1e-6
ROPE_BASE = 10000.0
PAST_LEN = 8192
PAGE_SIZE = 128

LANES = 128
SUBLANES = 8
MIB = 1024 * 1024

GM_HEAD_DIM = 128
GM_CHUNK = 128
SSM_HEAD_DIM = 64
SSM_GROUPS = 2
SSM_STATE = 128
SSM_CONV = 4
SSM_CHUNK = 128
NOPE_DIM = 128
ROPE_DIM = 64
V_DIM = 128
Q_BLOCK = 128

DT_LANE0 = ROPE_DIM


def _cparams(sem, vmem_mib):
    return pltpu.CompilerParams(dimension_semantics=sem, vmem_limit_bytes=int(vmem_mib * MIB))


def _nt(a, b):
    return lax.dot_general(a, b, (((1,), (1,)), ((), ())), preferred_element_type=F32)


def _dot(a, b):
    return jnp.dot(a, b, preferred_element_type=F32)


def _silu(x):
    return x * jax.nn.sigmoid(x)


def _softplus(x):
    return jnp.maximum(x, 0.0) + jnp.log1p(jnp.exp(-jnp.abs(x)))


def _gelu_tanh(x):
    return 0.5 * x * (1.0 + jnp.tanh(np.sqrt(2.0 / np.pi).astype(np.float32) * (x + 0.044715 * (x * x * x))))


def _rms(x, gain):
    return x * lax.rsqrt(jnp.mean(x * x, axis=-1, keepdims=True) + EPS) * gain


def _split3(x):
    hi = x.astype(BF16)
    r1 = x - hi.astype(F32)
    mid = r1.astype(BF16)
    lo = (r1 - mid.astype(F32)).astype(BF16)
    return hi, mid, lo


def _expand_heads(x, e3):
    return _dot(jnp.concatenate(_split3(x), axis=1), e3)


FFN_TF = 512
CAST_ROWS = 2048
FFN_DOWN_CHUNK = 1024
NORM_ROWS = 64


def _norm_rows(x_ref, g_ref, xn_ref, copy_ref=None):
    g = g_ref[...]
    for r in range(x_ref.shape[0] // NORM_ROWS):
        rows = slice(r * NORM_ROWS, (r + 1) * NORM_ROWS)
        x = x_ref[rows, :]
        xn_ref[rows, :] = _rms(x, g).astype(BF16)
        if copy_ref is not None:
            copy_ref[rows, :] = x


def _ffn_body(x_ref, g_ref, wgu_ref, wd_ref, o_ref, xn_ref):
    @pl.when(pl.program_id(1) == 0)
    def _():
        _norm_rows(x_ref, g_ref, xn_ref, o_ref)

    tf = wd_ref.shape[0]
    gu = _dot(xn_ref[...], wgu_ref[0])
    act = (_silu(gu[:, :tf]) * gu[:, tf:] * 0.5).astype(BF16)
    dc = FFN_DOWN_CHUNK
    for n in range(o_ref.shape[1] // dc):
        o_ref[:, n * dc:(n + 1) * dc] += _dot(act, wd_ref[:, n * dc:(n + 1) * dc])


def _ffn(h, g, wgu, wd, *, tm):
    m, d = h.shape
    nf, _, tf2 = wgu.shape
    tf = tf2 // 2
    vmem = (tm * d * 4 * 3 + tm * d * 2 + 3 * 2 * d * tf * 2 + 3 * tm * tf * 4 + 2 * tm * FFN_DOWN_CHUNK * 4) / MIB + 3
    return pl.pallas_call(
        _ffn_body,
        grid=(m // tm, nf),
        in_specs=[
            pl.BlockSpec((tm, d), lambda i, j: (i, 0), pipeline_mode=pl.Buffered(1)),
            pl.BlockSpec((1, d), lambda i, j: (0, 0)),
            pl.BlockSpec((1, d, tf2), lambda i, j: (j, 0, 0)),
            pl.BlockSpec((tf, d), lambda i, j: (j, 0)),
        ],
        out_specs=pl.BlockSpec((tm, d), lambda i, j: (i, 0)),
        out_shape=jax.ShapeDtypeStruct((m, d), F32),
        scratch_shapes=[pltpu.VMEM((tm, d), BF16)],
        compiler_params=_cparams(("parallel", "arbitrary"), vmem),
        name="ffn",
    )(h, g, wgu, wd)


def _cast_gate_up_body(wg_ref, wu_ref, o_ref):
    tf = wg_ref.shape[2]
    o_ref[0, :, :tf] = wg_ref[0].astype(BF16)
    o_ref[0, :, tf:] = wu_ref[0].astype(BF16)


def _cast_gate_up(wg_all, wu_all, *, layer, tf, tr):
    _, d, f = wg_all.shape
    spec = pl.BlockSpec((1, tr, tf), lambda j, r: (layer, r, j))
    return pl.pallas_call(
        _cast_gate_up_body,
        grid=(f // tf, d // tr),
        in_specs=[spec, spec],
        out_specs=pl.BlockSpec((1, tr, 2 * tf), lambda j, r: (j, r, 0)),
        out_shape=jax.ShapeDtypeStruct((f // tf, d, 2 * tf), BF16),
        compiler_params=_cparams(("parallel", "parallel"), (2 * 2 * tr * tf * 4 + 2 * tr * 2 * tf * 2) / MIB + 4),
        name="cast_gate_up",
    )(wg_all, wu_all)


def _cast_body(w_ref, o_ref):
    o_ref[...] = w_ref[0].astype(BF16)


def _cast(w_all, *, layer, tr):
    _, r, c = w_all.shape
    return pl.pallas_call(
        _cast_body,
        grid=(r // tr,),
        in_specs=[pl.BlockSpec((1, tr, c), lambda i: (layer, i, 0))],
        out_specs=pl.BlockSpec((tr, c), lambda i: (i, 0)),
        out_shape=jax.ShapeDtypeStruct((r, c), BF16),
        compiler_params=_cparams(("parallel",), (2 * tr * c * 4 + 2 * tr * c * 2) / MIB + 4),
        name="cast",
    )(w_all)


def _norm_mm_body(x_ref, g_ref, w_ref, o_ref, xn_ref):
    @pl.when(pl.program_id(1) == 0)
    def _():
        _norm_rows(x_ref, g_ref, xn_ref)

    o_ref[...] = _dot(xn_ref[...], w_ref[...])


def _norm_matmul(h, g, w, *, tm, tn):
    m, d = h.shape
    n = w.shape[1]
    vmem = (2 * tm * d * 4 + tm * d * 2 + 2 * d * tn * 2 + 3 * tm * tn * 4) / MIB + 4
    return pl.pallas_call(
        _norm_mm_body,
        grid=(m // tm, n // tn),
        in_specs=[
            pl.BlockSpec((tm, d), lambda i, j: (i, 0), pipeline_mode=pl.Buffered(1)),
            pl.BlockSpec((1, d), lambda i, j: (0, 0)),
            pl.BlockSpec((d, tn), lambda i, j: (0, j)),
        ],
        out_specs=pl.BlockSpec((tm, tn), lambda i, j: (i, j)),
        out_shape=jax.ShapeDtypeStruct((m, n), F32),
        scratch_shapes=[pltpu.VMEM((tm, d), BF16)],
        compiler_params=_cparams(("parallel", "arbitrary"), vmem),
        name="in_proj",
    )(h, g, w)


def _out_proj_body(gm_ref, ssm_p_ref, ssm_s_ref, mla_p_ref, mla_s_ref, w_ref, res_ref, o_ref, *, n_prompt_blocks):
    is_sample = pl.program_id(0) >= n_prompt_blocks
    ssm = jnp.where(is_sample, ssm_s_ref[...], ssm_p_ref[...])
    mla = jnp.where(is_sample, mla_s_ref[...], mla_p_ref[...])
    k1 = gm_ref.shape[1]
    k2 = k1 + ssm.shape[1]
    o_ref[...] = (res_ref[...] + _dot(gm_ref[...], w_ref[0:k1, :]) + _dot(ssm, w_ref[k1:k2, :])
                  + _dot(mla, w_ref[k2:, :]))


def _out_proj(h, gm, ssm_p, ssm_s, mla_p, mla_s, w, *, tm, tn):
    m, d = h.shape
    k = w.shape[0]
    npb = ssm_p.shape[0] // tm
    assert ssm_s.shape[0] == tm and mla_s.shape[0] == tm
    vmem = (2 * 2 * tm * k * 2 + 2 * k * tn * 2 + 6 * tm * tn * 4) / MIB + 4
    prompt = lambda width: pl.BlockSpec((tm, width), lambda i, j: (jnp.minimum(i, npb - 1), 0))
    sample = lambda width: pl.BlockSpec((tm, width), lambda i, j: (0, 0))
    return pl.pallas_call(
        functools.partial(_out_proj_body, n_prompt_blocks=npb),
        grid=(m // tm, d // tn),
        in_specs=[
            pl.BlockSpec((tm, gm.shape[1]), lambda i, j: (i, 0)),
            prompt(ssm_p.shape[1]), sample(ssm_s.shape[1]), prompt(mla_p.shape[1]), sample(mla_s.shape[1]),
            pl.BlockSpec((k, tn), lambda i, j: (0, j)),
            pl.BlockSpec((tm, tn), lambda i, j: (i, j)),
        ],
        out_specs=pl.BlockSpec((tm, tn), lambda i, j: (i, j)),
        out_shape=jax.ShapeDtypeStruct((m, d), F32),
        compiler_params=_cparams(("parallel", "arbitrary"), vmem),
        name="out_proj",
    )(gm, ssm_p, ssm_s, mla_p, mla_s, w, h)


def _ple_body(x_ref, g_ref, pe_ref, wg_ref, wp_ref, res_ref, o_ref, xn_ref):
    @pl.when(pl.program_id(1) == 0)
    def _():
        _norm_rows(x_ref, g_ref, xn_ref)

    gate = jax.nn.sigmoid(_dot(xn_ref[...], wg_ref[...]))
    proj = _dot(pe_ref[...].astype(BF16), wp_ref[...])
    o_ref[...] = res_ref[...] + gate * proj


def _ple(h, g, pe, wg, wp, *, tm, tn):
    m, d = h.shape
    dp = pe.shape[1]
    vmem = (2 * tm * d * 4 + tm * d * 2 + 2 * tm * dp * 4 + 2 * d * tn * 2 + 2 * dp * tn * 2 + 8 * tm * tn * 4) / MIB + 4
    return pl.pallas_call(
        _ple_body,
        grid=(m // tm, d // tn),
        in_specs=[
            pl.BlockSpec((tm, d), lambda i, j: (i, 0), pipeline_mode=pl.Buffered(1)),
            pl.BlockSpec((1, d), lambda i, j: (0, 0)),
            pl.BlockSpec((tm, dp), lambda i, j: (i, 0)),
            pl.BlockSpec((d, tn), lambda i, j: (0, j)),
            pl.BlockSpec((dp, tn), lambda i, j: (0, j)),
            pl.BlockSpec((tm, tn), lambda i, j: (i, j)),
        ],
        out_specs=pl.BlockSpec((tm, tn), lambda i, j: (i, j)),
        out_shape=jax.ShapeDtypeStruct((m, d), F32),
        scratch_shapes=[pltpu.VMEM((tm, d), BF16)],
        compiler_params=_cparams(("parallel", "arbitrary"), vmem),
        name="ple",
    )(h, g, pe, wg, wp, h)


def _sgu_body(u_ref, v_ref, gain_ref, w_ref, bias_ref, o_ref, vn_ref, *, heads):
    u = _gelu_tanh(u_ref[...])
    v = _gelu_tanh(v_ref[...])
    for g in range(heads):
        sl = slice(g * GM_HEAD_DIM, (g + 1) * GM_HEAD_DIM)
        vn = _rms(v[:, sl], gain_ref[:, sl])
        vn_ref[:, sl] = vn
        mixed = _dot(w_ref[0, g], vn.astype(BF16)) + bias_ref[0, :, sl]
        o_ref[:, sl] = (u[:, sl] * mixed).astype(o_ref.dtype)


def _sgu(a, gain, wbig, bias, *, rows, n_prompt_blocks):
    m = a.shape[0]
    width = gain.shape[1]
    heads = width // GM_HEAD_DIM
    sel = lambda i: jnp.minimum(i // n_prompt_blocks, 1)
    return pl.pallas_call(
        functools.partial(_sgu_body, heads=heads),
        grid=(m // rows,),
        in_specs=[
            pl.BlockSpec((rows, width), lambda i: (i, 0)),
            pl.BlockSpec((rows, width), lambda i: (i, 1)),
            pl.BlockSpec((1, width), lambda i: (0, 0)),
            pl.BlockSpec((1, heads, rows, rows), lambda i: (sel(i), 0, 0, 0)),
            pl.BlockSpec((1, rows, width), lambda i: (sel(i), 0, 0)),
        ],
        out_specs=[
            pl.BlockSpec((rows, width), lambda i: (i, 0)),
            pl.BlockSpec((rows, width), lambda i: (0, 0)),
        ],
        out_shape=[jax.ShapeDtypeStruct((m, width), BF16), jax.ShapeDtypeStruct((rows, width), F32)],
        compiler_params=_cparams(("arbitrary",), 40),
        name="sgu",
    )(a, a, gain, wbig, bias)


def _ssd_common_tail(y, xs, z, dfull, norm, o_ref, width):
    y = y + xs * dfull
    gt = y * _silu(z)
    gw = width // SSM_GROUPS
    for g in range(SSM_GROUPS):
        sl = slice(g * gw, (g + 1) * gw)
        seg = gt[:, sl]
        o_ref[:, sl] = (seg * lax.rsqrt(jnp.mean(seg * seg, axis=-1, keepdims=True) + EPS) * norm[:, sl]).astype(o_ref.dtype)


def _dt_slab(slab_ref, dtb_ref, alog_ref, heads):
    lane = lax.broadcasted_iota(jnp.int32, slab_ref.shape, 1)
    raw = jnp.where((lane >= DT_LANE0) & (lane < DT_LANE0 + heads), slab_ref[...], 0.0)
    dt = _softplus(raw + dtb_ref[...])
    return dt, dt * (-jnp.exp(alog_ref[...]))


def _ssd_prompt_body(z_ref, xbc_ref, slab_ref, cw_ref, cb_ref, dtb_ref, alog_ref, dfull_ref, norm_ref, tril3_ref,
                     e3_ref, o_ref, st_out_ref, conv_out_ref, full_ref, state_ref, y_ref, *, chunk, n_chunks, heads):
    c = pl.program_id(1)
    width = heads * SSM_HEAD_DIM
    hpg = heads // SSM_GROUPS
    gw = hpg * SSM_HEAD_DIM
    tail = SUBLANES

    @pl.when(c == 0)
    def _():
        full_ref[0:tail, :] = jnp.zeros((tail, full_ref.shape[1]), F32)
        state_ref[...] = jnp.zeros(state_ref.shape, F32)

    full_ref[tail:tail + chunk, :] = xbc_ref[...]
    conv = cb_ref[...]
    for k in range(SSM_CONV):
        conv = conv + cw_ref[k:k + 1, :] * full_ref[pl.ds(tail - (SSM_CONV - 1) + k, chunk), :]
    full_ref[0:tail, :] = full_ref[chunk:chunk + tail, :]
    xc = _silu(conv)
    xs = xc[:, :width]
    bm = xc[:, width:width + SSM_GROUPS * SSM_STATE].astype(BF16)
    cm = xc[:, width + SSM_GROUPS * SSM_STATE:].astype(BF16)

    dt, a = _dt_slab(slab_ref, dtb_ref, alog_ref, heads)
    acs = _dot(tril3_ref[...], jnp.concatenate(_split3(a), axis=0))
    acs_t = acs.T
    arem = acs[chunk - 1:chunk, :] - acs
    e3 = e3_ref[...]
    xdt = xs * _expand_heads(dt, e3)
    ecs = jnp.exp(_expand_heads(acs, e3))
    xw_t = (xdt * jnp.exp(_expand_heads(arem, e3))).T.astype(BF16)
    ecol = ecs.T[:, chunk - 1:chunk]
    xdt_b = xdt.astype(BF16)

    row_i = lax.broadcasted_iota(jnp.int32, (chunk, chunk), 0)
    col_i = lax.broadcasted_iota(jnp.int32, (chunk, chunk), 1)
    causal = row_i >= col_i
    for g in range(SSM_GROUPS):
        cg = cm[:, g * SSM_STATE:(g + 1) * SSM_STATE]
        bg = bm[:, g * SSM_STATE:(g + 1) * SSM_STATE]
        cb = _nt(cg, bg)
        st = state_ref[g * hpg:(g + 1) * hpg].reshape(gw, SSM_STATE)
        y_ref[:, g * gw:(g + 1) * gw] = _nt(cg, st.astype(BF16)) * ecs[:, g * gw:(g + 1) * gw]
        for hh in range(hpg):
            h = g * hpg + hh
            col = acs[:, DT_LANE0 + h:DT_LANE0 + h + 1]
            row = acs_t[DT_LANE0 + h:DT_LANE0 + h + 1, :]
            mat = jnp.where(causal, cb * jnp.exp(col - row), 0.0).astype(BF16)
            hs = slice(h * SSM_HEAD_DIM, (h + 1) * SSM_HEAD_DIM)
            y_ref[:, hs] += _dot(mat, xdt_b[:, hs])
        s_new = _dot(xw_t[g * gw:(g + 1) * gw, :], bg)
        state_ref[g * hpg:(g + 1) * hpg] = (st * ecol[g * gw:(g + 1) * gw, :] + s_new).reshape(hpg, SSM_HEAD_DIM, SSM_STATE)

    _ssd_common_tail(y_ref[...], xs, z_ref[...], dfull_ref[...], norm_ref[...], o_ref, width)

    @pl.when(c == n_chunks - 1)
    def _():
        st_out_ref[0] = state_ref[...]
        conv_out_ref[0] = full_ref[0:tail, :]


def _ssd_prompt(a, params, *, batch, seq, col_z, col_xbc, col_slab):
    cw, cb, dtb, alog, dfull, norm, tril3, e3 = params
    width = norm.shape[1]
    heads = width // SSM_HEAD_DIM
    conv_ch = cw.shape[1]
    chunk = SSM_CHUNK
    nc = seq // chunk
    const = lambda shape: pl.BlockSpec(shape, lambda b, c: (0,) * len(shape))
    rowblk = lambda b, c: b * nc + c
    return pl.pallas_call(
        functools.partial(_ssd_prompt_body, chunk=chunk, n_chunks=nc, heads=heads),
        grid=(batch, nc),
        in_specs=[
            pl.BlockSpec((chunk, width), lambda b, c: (rowblk(b, c), col_z // width)),
            pl.BlockSpec((chunk, conv_ch), lambda b, c: (rowblk(b, c), col_xbc // conv_ch)),
            pl.BlockSpec((chunk, LANES), lambda b, c: (rowblk(b, c), col_slab // LANES)),
            const(cw.shape), const(cb.shape), const(dtb.shape), const(alog.shape), const(dfull.shape),
            const(norm.shape), const(tril3.shape), const(e3.shape),
        ],
        out_specs=[
            pl.BlockSpec((chunk, width), lambda b, c: (rowblk(b, c), 0)),
            pl.BlockSpec((1, heads, SSM_HEAD_DIM, SSM_STATE), lambda b, c: (b, 0, 0, 0)),
            pl.BlockSpec((1, SUBLANES, conv_ch), lambda b, c: (b, 0, 0)),
        ],
        out_shape=[
            jax.ShapeDtypeStruct((batch * seq, width), BF16),
            jax.ShapeDtypeStruct((batch, heads, SSM_HEAD_DIM, SSM_STATE), F32),
            jax.ShapeDtypeStruct((batch, SUBLANES, conv_ch), F32),
        ],
        scratch_shapes=[
            pltpu.VMEM((chunk + SUBLANES, conv_ch), F32),
            pltpu.VMEM((heads, SSM_HEAD_DIM, SSM_STATE), F32),
            pltpu.VMEM((chunk, width), F32),
        ],
        compiler_params=_cparams(("parallel", "arbitrary"), 40),
        name="ssd_prompt",
    )(a, a, a, cw, cb, dtb, alog, dfull, norm, tril3, e3)


def _ssd_sample_body(z_ref, xbc_ref, slab_ref, cs_ref, st_ref, cw_ref, cb_ref, dtb_ref, alog_ref, dfull_ref, norm_ref,
                     e3_ref, o_ref, st_out_ref, conv_out_ref, *, rows, steps, heads):
    width = heads * SSM_HEAD_DIM
    hpg = heads // SSM_GROUPS
    gw = hpg * SSM_HEAD_DIM
    nreq = rows // steps
    x = xbc_ref[...]
    cs = cs_ref[...]
    t_of = lambda shape: lax.broadcasted_iota(jnp.int32, shape, 0) % steps

    def back(v, j):
        return v if j == 0 else pltpu.roll(v, j, axis=0)

    def fwd(v, j):
        return v if j == 0 else pltpu.roll(v, rows - j, axis=0)

    tx = t_of(x.shape)
    conv = cb_ref[...] + cw_ref[SSM_CONV - 1:SSM_CONV, :] * x
    for k in range(SSM_CONV - 1):
        j = SSM_CONV - 1 - k
        conv = conv + cw_ref[k:k + 1, :] * jnp.where(tx >= j, back(x, j), fwd(cs, k))
    conv_out_ref[...] = fwd(x, steps - (SSM_CONV - 1))
    xc = _silu(conv)
    xs = xc[:, :width]
    bm = xc[:, width:width + SSM_GROUPS * SSM_STATE]
    cm = xc[:, width + SSM_GROUPS * SSM_STATE:]

    dt, a = _dt_slab(slab_ref, dtb_ref, alog_ref, heads)
    ts = t_of(a.shape)
    acs = a
    arem = jnp.zeros_like(a)
    for j in range(1, steps):
        acs = acs + jnp.where(ts >= j, back(a, j), 0.0)
        arem = arem + jnp.where(ts < steps - j, fwd(a, j), 0.0)
    e3 = e3_ref[...]
    acsx = _expand_heads(acs, e3)
    xdt = xs * _expand_heads(dt, e3)
    ecs = jnp.exp(acsx)
    xw = xdt * jnp.exp(_expand_heads(arem, e3))

    tw = t_of(xs.shape)
    y = jnp.zeros_like(xs)
    for j in range(steps):
        bj = back(bm, j)
        dec = jnp.exp(acsx - back(acsx, j)) * back(xdt, j)
        parts = []
        for g in range(SSM_GROUPS):
            sl = slice(g * SSM_STATE, (g + 1) * SSM_STATE)
            cbj = jnp.sum(cm[:, sl] * bj[:, sl], axis=-1, keepdims=True)
            parts.append(cbj * dec[:, g * gw:(g + 1) * gw])
        y = y + jnp.where(tw >= j, jnp.concatenate(parts, axis=1), 0.0)

    pad = jnp.zeros((LANES - rows, width), F32)
    xw_t = jnp.concatenate([xw, pad], axis=0).T.astype(BF16)
    e_t = jnp.concatenate([ecs, pad], axis=0).T
    cm_b = cm.astype(BF16)
    bm_pad = jnp.concatenate([bm, jnp.zeros((LANES - rows, bm.shape[1]), F32)], axis=0)
    req_y = lax.broadcasted_iota(jnp.int32, (rows, gw), 0) // steps
    req_b = lax.broadcasted_iota(jnp.int32, (LANES, SSM_STATE), 0) // steps
    yoff = [jnp.zeros((rows, gw), F32) for _ in range(SSM_GROUPS)]
    for b in range(nreq):
        last = b * steps + steps - 1
        for g in range(SSM_GROUPS):
            sl = slice(g * SSM_STATE, (g + 1) * SSM_STATE)
            h0 = st_ref[0, b, g * hpg:(g + 1) * hpg].reshape(gw, SSM_STATE)
            yoff[g] = yoff[g] + jnp.where(req_y == b, _nt(cm_b[:, sl], h0.astype(BF16)), 0.0)
            b_only = jnp.where(req_b == b, bm_pad[:, sl], 0.0).astype(BF16)
            s_new = _dot(xw_t[g * gw:(g + 1) * gw, :], b_only)
            decay = e_t[g * gw:(g + 1) * gw, last:last + 1]
            st_out_ref[b, g * hpg:(g + 1) * hpg] = (h0 * decay + s_new).reshape(hpg, SSM_HEAD_DIM, SSM_STATE)
    y = y + jnp.concatenate(yoff, axis=1) * ecs

    _ssd_common_tail(y, xs, z_ref[...], dfull_ref[...], norm_ref[...], o_ref, width)


def _ssd_sample(a_s, cs4, state, params, *, layer, steps, nreq_blk, col_z, col_xbc, col_slab):
    cw, cb, dtb, alog, dfull, norm, _, e3 = params
    width = norm.shape[1]
    heads = width // SSM_HEAD_DIM
    conv_ch = cw.shape[1]
    m = a_s.shape[0]
    rows = nreq_blk * steps
    const = lambda shape: pl.BlockSpec(shape, lambda i: (0,) * len(shape))
    st_spec = pl.BlockSpec((nreq_blk, heads, SSM_HEAD_DIM, SSM_STATE), lambda i: (i, 0, 0, 0))
    st_in_spec = pl.BlockSpec((1, nreq_blk, heads, SSM_HEAD_DIM, SSM_STATE), lambda i: (layer, i, 0, 0, 0))
    return pl.pallas_call(
        functools.partial(_ssd_sample_body, rows=rows, steps=steps, heads=heads),
        grid=(m // rows,),
        in_specs=[
            pl.BlockSpec((rows, width), lambda i: (i, col_z // width)),
            pl.BlockSpec((rows, conv_ch), lambda i: (i, col_xbc // conv_ch)),
            pl.BlockSpec((rows, LANES), lambda i: (i, col_slab // LANES)),
            pl.BlockSpec((rows, conv_ch), lambda i: (i, 0)),
            st_in_spec,
            const(cw.shape), const(cb.shape), const(dtb.shape), const(alog.shape), const(dfull.shape),
            const(norm.shape), const(e3.shape),
        ],
        out_specs=[
            pl.BlockSpec((rows, width), lambda i: (i, 0)),
            st_spec,
            pl.BlockSpec((rows, conv_ch), lambda i: (i, 0)),
        ],
        out_shape=[
            jax.ShapeDtypeStruct((m, width), BF16),
            jax.ShapeDtypeStruct(state.shape[1:], F32),
            jax.ShapeDtypeStruct((m, conv_ch), F32),
        ],
        compiler_params=_cparams(("parallel",), 48),
        name="ssd_sample",
    )(a_s, a_s, a_s, cs4, state, cw, cb, dtb, alog, dfull, norm, e3)


def _rms_pairs(x, bd):
    s = x * x
    hi = s.astype(BF16)
    lo = (s - hi.astype(F32)).astype(BF16)
    ssq = _dot(hi, bd) + _dot(lo, bd)
    return x * lax.rsqrt(ssq * (1.0 / ROPE_DIM) + EPS)


def _rope_pairs(y, cos_t, sin_s):
    lane = lax.broadcasted_iota(jnp.int32, y.shape, 1)
    half = ROPE_DIM // 2
    rot = jnp.where((lane % ROPE_DIM) < half, pltpu.roll(y, LANES - half, axis=1), pltpu.roll(y, half, axis=1))
    return y * cos_t + rot * sin_s


def _mla_proj_body(cq_ref, ckv_ref, slab_ref, cos_ref, sin_ref, gq_ref, wuq_ref, gqn_ref, gqr_ref, gkv_ref, gkr_ref,
                   bd_ref, qn_ref, qr_ref, c_ref, kr_ref, kr2_ref, *, heads, qscale):
    cqn = _rms(cq_ref[...], gq_ref[...]).astype(BF16)
    q = _dot(cqn, wuq_ref[...])
    gqn = gqn_ref[...]
    for h in range(heads):
        sl = slice(h * NOPE_DIM, (h + 1) * NOPE_DIM)
        qn_ref[:, sl] = (_rms(q[:, sl], gqn) * qscale).astype(BF16)
    bd = bd_ref[...]
    cos_t = cos_ref[...]
    sin_s = sin_ref[...]
    base = heads * NOPE_DIM
    for j in range(heads * ROPE_DIM // LANES):
        x = q[:, base + j * LANES: base + (j + 1) * LANES]
        y = _rope_pairs(_rms_pairs(x, bd) * gqr_ref[...], cos_t, sin_s)
        qr_ref[:, j * LANES:(j + 1) * LANES] = (y * qscale).astype(BF16)
    c_ref[...] = _rms(ckv_ref[...], gkv_ref[...])
    kr = _rope_pairs(_rms_pairs(slab_ref[...], bd) * gkr_ref[...], cos_t, sin_s)
    kr_ref[...] = kr
    kr2_ref[...] = (kr + pltpu.roll(kr, ROPE_DIM, axis=1)).astype(BF16)


def _mla_proj(a, cos_t, sin_s, params, *, tm, col_cq, col_ckv, col_slab, qscale):
    gq, wuq, gqn, gqr, gkv, gkr, bd = params
    m = a.shape[0]
    q_rank = gq.shape[1]
    kv_rank = gkv.shape[1]
    heads = wuq.shape[1] // (NOPE_DIM + ROPE_DIM)
    const = lambda shape: pl.BlockSpec(shape, lambda i: (0,) * len(shape))
    row = lambda w: pl.BlockSpec((tm, w), lambda i: (i, 0))
    return pl.pallas_call(
        functools.partial(_mla_proj_body, heads=heads, qscale=qscale),
        grid=(m // tm,),
        in_specs=[
            pl.BlockSpec((tm, q_rank), lambda i: (i, col_cq // q_rank)),
            pl.BlockSpec((tm, kv_rank), lambda i: (i, col_ckv // kv_rank)),
            pl.BlockSpec((tm, LANES), lambda i: (i, col_slab // LANES)),
            row(LANES), row(LANES),
            const(gq.shape), const(wuq.shape), const(gqn.shape), const(gqr.shape), const(gkv.shape),
            const(gkr.shape), const(bd.shape),
        ],
        out_specs=[row(heads * NOPE_DIM), row(heads * ROPE_DIM), row(kv_rank), row(LANES), row(LANES)],
        out_shape=[
            jax.ShapeDtypeStruct((m, heads * NOPE_DIM), BF16),
            jax.ShapeDtypeStruct((m, heads * ROPE_DIM), BF16),
            jax.ShapeDtypeStruct((m, kv_rank), F32),
            jax.ShapeDtypeStruct((m, LANES), F32),
            jax.ShapeDtypeStruct((m, LANES), BF16),
        ],
        compiler_params=_cparams(("parallel",), 48),
        name="mla_proj",
    )(a, a, a, cos_t, sin_s, gq, wuq, gqn, gqr, gkv, gkr, bd)


def _kv_proj_body(c_ref, kr2_ref, wuk_ref, wuv_ref, gkn_ref, k_ref, v_ref, *, heads):
    cb = c_ref[...].astype(BF16)
    k = _dot(cb, wuk_ref[...])
    gkn = gkn_ref[...]
    kr2 = kr2_ref[...]
    for h in range(heads):
        sl = slice(h * NOPE_DIM, (h + 1) * NOPE_DIM)
        k_ref[:, 2 * h * NOPE_DIM:(2 * h + 1) * NOPE_DIM] = _rms(k[:, sl], gkn).astype(BF16)
        k_ref[:, (2 * h + 1) * NOPE_DIM:(2 * h + 2) * NOPE_DIM] = kr2
    v_ref[...] = _dot(cb, wuv_ref[...]).astype(BF16)


def _kv_proj(c, kr2, wuk, wuv, gkn, *, rows, tk):
    kv_rank = c.shape[1]
    n = wuk.shape[1]
    heads = n // NOPE_DIM
    const = lambda shape: pl.BlockSpec(shape, lambda i: (0,) * len(shape))
    return pl.pallas_call(
        functools.partial(_kv_proj_body, heads=heads),
        grid=(rows // tk,),
        in_specs=[pl.BlockSpec((tk, kv_rank), lambda i: (i, 0)), pl.BlockSpec((tk, LANES), lambda i: (i, 0)),
                  const(wuk.shape), const(wuv.shape), const(gkn.shape)],
        out_specs=[pl.BlockSpec((tk, 2 * n), lambda i: (i, 0)), pl.BlockSpec((tk, n), lambda i: (i, 0))],
        out_shape=[jax.ShapeDtypeStruct((rows, 2 * n), BF16), jax.ShapeDtypeStruct((rows, n), BF16)],
        compiler_params=_cparams(("parallel",), 32),
        name="kv_proj",
    )(c, kr2, wuk, wuv, gkn)


FLASH_HEADS = 2


def _flash_body(qn_ref, qr_ref, k_ref, v_ref, o_ref, m_ref, l_ref, acc_ref, *, blk):
    i = pl.program_id(2)
    lane = lax.broadcasted_iota(jnp.int32, qr_ref.shape, 1)
    zero = jnp.zeros_like(qr_ref[...])
    q = []
    for hh in range(FLASH_HEADS):
        qr = jnp.where((lane // ROPE_DIM) == hh, qr_ref[...], zero)
        q.append(jnp.concatenate([qn_ref[:, hh * NOPE_DIM:(hh + 1) * NOPE_DIM], qr], axis=1))
    m_ref[...] = jnp.full(m_ref.shape, -jnp.inf, F32)
    l_ref[...] = jnp.zeros(l_ref.shape, F32)
    acc_ref[...] = jnp.zeros(acc_ref.shape, F32)
    kw = NOPE_DIM + LANES
    reps = blk // LANES

    def step(j, masked):
        rows = pl.ds(pl.multiple_of(j * blk, blk), blk)
        for hh in range(FLASH_HEADS):
            s = _nt(q[hh], k_ref[rows, hh * kw:(hh + 1) * kw])
            if masked:
                r = lax.broadcasted_iota(jnp.int32, s.shape, 0)
                c = lax.broadcasted_iota(jnp.int32, s.shape, 1)
                s = jnp.where(c <= r, s, -jnp.inf)
            m_old = m_ref[hh]
            m_new = jnp.maximum(m_old, jnp.max(s, axis=-1, keepdims=True))
            corr = jnp.exp2(m_old - m_new)
            p = jnp.exp2(s - jnp.tile(m_new, (1, reps)))
            l_ref[hh] = l_ref[hh] * corr + jnp.sum(p, axis=-1, keepdims=True)
            acc_ref[hh] = acc_ref[hh] * corr + _dot(p.astype(BF16), v_ref[rows, hh * V_DIM:(hh + 1) * V_DIM])
            m_ref[hh] = m_new

    def body(j, carry):
        step(j, False)
        return carry

    lax.fori_loop(0, i, body, 0)
    step(i, True)
    for hh in range(FLASH_HEADS):
        o_ref[:, hh * V_DIM:(hh + 1) * V_DIM] = (acc_ref[hh] / l_ref[hh]).astype(o_ref.dtype)


def _flash(qn, qr, kcat, v, *, batch, seq, blk):
    heads = v.shape[1] // V_DIM
    nb = seq // blk
    hp = FLASH_HEADS
    return pl.pallas_call(
        functools.partial(_flash_body, blk=blk),
        grid=(batch, heads // hp, nb),
        in_specs=[
            pl.BlockSpec((blk, hp * NOPE_DIM), lambda b, h, i: (b * nb + i, h)),
            pl.BlockSpec((blk, LANES), lambda b, h, i: (b * nb + i, h)),
            pl.BlockSpec((seq, hp * (NOPE_DIM + LANES)), lambda b, h, i: (b, h)),
            pl.BlockSpec((seq, hp * V_DIM), lambda b, h, i: (b, h)),
        ],
        out_specs=pl.BlockSpec((blk, hp * V_DIM), lambda b, h, i: (b * nb + i, h)),
        out_shape=jax.ShapeDtypeStruct((batch * seq, heads * V_DIM), BF16),
        scratch_shapes=[pltpu.VMEM((hp, blk, LANES), F32), pltpu.VMEM((hp, blk, LANES), F32),
                        pltpu.VMEM((hp, blk, V_DIM), F32)],
        compiler_params=_cparams(("parallel", "parallel", "arbitrary"), 32),
        name="flash",
    )(qn, qr, kcat, v)


def _absorb_body(qn_ref, w_ref, o_ref):
    o_ref[0] = _dot(qn_ref[...], w_ref[0]).astype(o_ref.dtype)


def _absorb(qn, w, *, row0, rows):
    heads, _, kv_rank = w.shape
    return pl.pallas_call(
        _absorb_body,
        grid=(heads,),
        in_specs=[
            pl.BlockSpec((rows, NOPE_DIM), lambda h: (row0 // rows, h)),
            pl.BlockSpec((1, NOPE_DIM, kv_rank), lambda h: (h, 0, 0)),
        ],
        out_specs=pl.BlockSpec((1, rows, kv_rank), lambda h: (h, 0, 0)),
        out_shape=jax.ShapeDtypeStruct((heads, rows, kv_rank), F32),
        compiler_params=_cparams(("parallel",), 16),
        name="absorb",
    )(qn, w)


def _v_up_body(o_ref_in, w_ref, o_ref):
    o_ref[...] = _dot(o_ref_in[0].astype(BF16), w_ref[0]).astype(o_ref.dtype)


def _v_up(o_lat, w):
    heads, rows, kv_rank = o_lat.shape
    return pl.pallas_call(
        _v_up_body,
        grid=(heads,),
        in_specs=[
            pl.BlockSpec((1, rows, kv_rank), lambda h: (h, 0, 0)),
            pl.BlockSpec((1, kv_rank, V_DIM), lambda h: (h, 0, 0)),
        ],
        out_specs=pl.BlockSpec((rows, V_DIM), lambda h: (0, h)),
        out_shape=jax.ShapeDtypeStruct((rows, heads * V_DIM), BF16),
        compiler_params=_cparams(("parallel",), 16),
        name="v_up",
    )(o_lat, w)


KEY_CHUNK = 512
PAGED_SPLITS = 1
PAGED_PAGES = 32


def _paged_body(pt_ref, qp_ref, qr_ref, cnew_ref, krnew_ref, wuk_ref, ones_ref, *rest, pages, heads, steps):
    c_refs = rest[:pages]
    krt_refs = rest[pages:2 * pages]
    o_ref, m_ref, l_ref, acc_ref, cpad_ref, krpad_ref = rest[2 * pages:]
    j = pl.program_id(1)
    nj = pl.num_programs(1)
    qrows = heads * steps
    qp = qp_ref[0].astype(BF16)
    qr = qr_ref[0].astype(BF16)

    @pl.when(j == 0)
    def _():
        m_ref[...] = jnp.full(m_ref.shape, -jnp.inf, F32)
        l_ref[...] = jnp.zeros(l_ref.shape, F32)
        acc_ref[...] = jnp.zeros(acc_ref.shape, F32)

    def attend(c_blk, krt_blk, valid):
        folded = None
        for t in range(wuk_ref.shape[1] // KEY_CHUNK):
            k = _dot(c_blk, wuk_ref[:, t * KEY_CHUNK:(t + 1) * KEY_CHUNK])
            sq = k * k
            for u in range(KEY_CHUNK // LANES):
                tile = sq[:, u * LANES:(u + 1) * LANES]
                folded = tile if folded is None else folded + tile
        hi = folded.astype(BF16)
        lo = (folded - hi.astype(F32)).astype(BF16)
        ssq = _dot(hi, ones_ref[...]) + _dot(lo, ones_ref[...])
        rinv_t = lax.rsqrt(ssq * (1.0 / NOPE_DIM) + EPS).T[:qrows, :]
        s = _nt(qp, c_blk) * rinv_t + _dot(qr, krt_blk)
        if valid is not None:
            s = jnp.where(valid, s, -jnp.inf)
        m_old = m_ref[...]
        m_new = jnp.maximum(m_old, jnp.max(s, axis=-1, keepdims=True))
        corr = jnp.exp2(m_old - m_new)
        p = jnp.exp2(s - m_new)
        l_ref[...] = l_ref[...] * corr + jnp.sum(p, axis=-1, keepdims=True)
        acc_ref[...] = acc_ref[...] * corr + _dot(p.astype(BF16), c_blk)
        m_ref[...] = m_new

    half = pages // PAGED_SPLITS
    for g in range(PAGED_SPLITS):
        ks = range(g * half, (g + 1) * half)
        c_blk = jnp.concatenate([c_refs[k][0, 0] for k in ks], axis=0).astype(BF16)
        krt_blk = jnp.concatenate([krt_refs[k][0, 0] for k in ks], axis=1).astype(BF16)
        attend(c_blk, krt_blk, None)

    @pl.when(j == nj - 1)
    def _():
        n = cpad_ref.shape[0]
        cpad_ref[...] = jnp.zeros(cpad_ref.shape, F32)
        krpad_ref[...] = jnp.zeros(krpad_ref.shape, F32)
        cpad_ref[0:steps, :] = cnew_ref[0]
        krpad_ref[0:steps, :] = krnew_ref[0]
        t = lax.broadcasted_iota(jnp.int32, (qrows, n), 0) % steps
        p = lax.broadcasted_iota(jnp.int32, (qrows, n), 1)
        krt_new = krpad_ref[...].T[:ROPE_DIM, :].astype(BF16)
        attend(cpad_ref[...].astype(BF16), krt_new, (p < steps) & (p <= t))
        o_ref[0] = acc_ref[...] / l_ref[...]


def _paged(page_table, qp, qr, c_new, kr_new, wuk, ones8, pool_c, pool_rt, *, layer, pages):
    nreq, qrows, kv_rank = qp.shape
    steps = c_new.shape[1]
    heads = qrows // steps
    n_pages = page_table.shape[1]
    assert qrows % SUBLANES == 0 and qrows <= LANES and n_pages % pages == 0 and pages % PAGED_SPLITS == 0
    flat_pt = page_table.reshape(-1)

    def page_spec(shape, k):
        return pl.BlockSpec((1, 1) + shape, lambda b, j, pt: (layer, pt[b * n_pages + j * pages + k], 0, 0))

    const = lambda shape: pl.BlockSpec(shape, lambda b, j, pt: (0,) * len(shape))
    grid_spec = pltpu.PrefetchScalarGridSpec(
        num_scalar_prefetch=1,
        grid=(nreq, n_pages // pages),
        in_specs=[
            pl.BlockSpec((1, qrows, kv_rank), lambda b, j, pt: (b, 0, 0)),
            pl.BlockSpec((1, qrows, ROPE_DIM), lambda b, j, pt: (b, 0, 0)),
            pl.BlockSpec((1, steps, kv_rank), lambda b, j, pt: (b, 0, 0)),
            pl.BlockSpec((1, steps, LANES), lambda b, j, pt: (b, 0, 0)),
            const(wuk.shape), const(ones8.shape),
        ] + [page_spec((PAGE_SIZE, kv_rank), k) for k in range(pages)]
          + [page_spec((ROPE_DIM, PAGE_SIZE), k) for k in range(pages)],
        out_specs=pl.BlockSpec((1, qrows, kv_rank), lambda b, j, pt: (b, 0, 0)),
        scratch_shapes=[
            pltpu.VMEM((qrows, 1), F32), pltpu.VMEM((qrows, 1), F32), pltpu.VMEM((qrows, kv_rank), F32),
            pltpu.VMEM((LANES, kv_rank), F32), pltpu.VMEM((LANES, LANES), F32),
        ],
    )
    return pl.pallas_call(
        functools.partial(_paged_body, pages=pages, heads=heads, steps=steps),
        grid_spec=grid_spec,
        out_shape=jax.ShapeDtypeStruct((nreq, qrows, kv_rank), F32),
        compiler_params=_cparams(("parallel", "arbitrary"), 48),
        name="paged",
    )(flat_pt, qp, qr, c_new, kr_new, wuk, ones8, *([pool_c] * pages), *([pool_rt] * pages))


def _rope_tables(positions):
    half = ROPE_DIM // 2
    inv = ROPE_BASE ** (-jnp.arange(half, dtype=F32) / half)
    ang = positions[:, None] * inv[None, :]
    cos = jnp.cos(ang).astype(F32)
    sin = jnp.sin(ang).astype(F32)
    reps = LANES // ROPE_DIM
    return jnp.tile(cos, (1, 2 * reps)), jnp.tile(jnp.concatenate([-sin, sin], axis=1), (1, reps))


def _slab(vec, lane0):
    return jnp.zeros((1, LANES), F32).at[0, lane0:lane0 + vec.shape[0]].set(vec.astype(F32))


def kernel(x_prompt, x_sample, cache_mla_latent, cache_mla_rope, state_ssm, state_conv, page_table, p_prompt, p_sample, norm_ffn1, w_ffn1_gate, w_ffn1_up, w_ffn1_down, norm_mix, w_in, gm_norm_v, gm_w_s, gm_b_s, ssm_conv_w, ssm_conv_b, ssm_dt_bias, ssm_a_log, ssm_d, ssm_norm, mla_q_norm, mla_w_uq, mla_qn_norm, mla_qr_norm, mla_kv_norm, mla_kr_norm, mla_w_uk, mla_kn_norm, mla_w_uv, w_out, norm_ffn2, w_ffn2_gate, w_ffn2_up, w_ffn2_down, norm_ple, w_ple_gate, w_ple_proj):
    bp, tp, d_model = x_prompt.shape
    bs, ts, _ = x_sample.shape
    depth = norm_ffn1.shape[0]
    mp, ms = bp * tp, bs * ts
    gm_width = gm_norm_v.shape[1]
    gm_heads = gm_width // GM_HEAD_DIM
    ssm_width = ssm_norm.shape[1]
    ssm_heads = ssm_width // SSM_HEAD_DIM
    conv_ch = ssm_conv_w.shape[2]
    q_rank = mla_q_norm.shape[1]
    kv_rank = mla_kv_norm.shape[1]
    mla_heads = mla_w_uq.shape[2]
    qscale = float((NOPE_DIM + ROPE_DIM) ** -0.5 * np.log2(np.e))

    tm = ms
    col_v = gm_width
    col_z = 2 * gm_width
    col_xbc = col_z + ssm_width
    col_cq = col_xbc + conv_ch
    col_ckv = col_cq + q_rank
    col_slab = col_ckv + kv_rank
    n_in = col_slab + LANES
    dt0 = col_xbc + conv_ch
    assert col_v == gm_width and col_z % ssm_width == 0 and col_xbc % conv_ch == 0
    assert col_cq % q_rank == 0 and col_ckv % kv_rank == 0 and col_slab % LANES == 0
    assert mp % tm == 0 and ms == tm and tp % SSM_CHUNK == 0 and ts < SUBLANES

    pos = jnp.concatenate([jnp.tile(jnp.arange(tp, dtype=F32), bp), jnp.tile(PAST_LEN + jnp.arange(ts, dtype=F32), bs)])
    cos_t, sin_s = _rope_tables(pos)
    bd = jnp.asarray(np.kron(np.eye(LANES // ROPE_DIM), np.ones((ROPE_DIM, ROPE_DIM))), BF16)
    tril = np.tril(np.ones((SSM_CHUNK, SSM_CHUNK)))
    tril3 = jnp.asarray(np.concatenate([tril] * 3, axis=1), BF16)
    e1 = np.zeros((LANES, ssm_width))
    for hh in range(ssm_heads):
        e1[DT_LANE0 + hh, hh * SSM_HEAD_DIM:(hh + 1) * SSM_HEAD_DIM] = 1.0
    e3 = jnp.asarray(np.concatenate([e1] * 3, axis=0), BF16)
    assert LANES % mla_heads == 0 and mla_heads * ts <= LANES
    ones8 = np.zeros((LANES, LANES))
    ones8[:, :mla_heads * ts] = np.kron(np.ones((LANES // mla_heads, 1)), np.kron(np.eye(mla_heads), np.ones((1, ts))))
    ones8 = jnp.asarray(ones8, BF16)
    pool_rt = jnp.swapaxes(cache_mla_rope, 2, 3)

    h = jnp.concatenate([x_prompt.reshape(mp, d_model), x_sample.reshape(ms, d_model)], axis=0)
    outs = [[] for _ in range(9)]
    n_prompt_blocks = mp // tm
    eye_c = jnp.eye(tm // GM_CHUNK, dtype=F32)
    eye_s = jnp.eye(bs, dtype=F32)

    for i in range(depth):
        bf = lambda w: w.astype(BF16)
        row = lambda v: v.reshape(1, -1).astype(F32)

        wgu = _cast_gate_up(w_ffn1_gate, w_ffn1_up, layer=i, tf=FFN_TF, tr=CAST_ROWS)
        h = _ffn(h, row(norm_ffn1[i]), wgu, _cast(w_ffn1_down, layer=i, tr=CAST_ROWS // 4), tm=tm)
        wi = w_in[i]
        w_in_p = jnp.concatenate(
            [wi[:, :dt0], wi[:, dt0 + ssm_heads:], wi[:, dt0:dt0 + ssm_heads],
             jnp.zeros((d_model, n_in - wi.shape[1]), wi.dtype)], axis=1)
        a = _norm_matmul(h, row(norm_mix[i]), bf(w_in_p), tm=tm, tn=n_in // 5)

        ws = gm_w_s[i]
        tril_c = jnp.tril(jnp.ones((GM_CHUNK, GM_CHUNK), F32))
        w_prompt = jnp.einsum('ab,gts->gatbs', eye_c, ws * tril_c).reshape(gm_heads, tm, tm)
        w_samp = jnp.einsum('ab,gts->gatbs', eye_s, ws[:, :ts, :ts] * tril_c[:ts, :ts]).reshape(gm_heads, tm, tm)
        wbig = bf(jnp.stack([w_prompt, w_samp]))
        bias_p = jnp.repeat(jnp.tile(gm_b_s[i].T, (tm // GM_CHUNK, 1)), GM_HEAD_DIM, axis=1)
        bias_s = jnp.repeat(jnp.tile(gm_b_s[i][:, :ts].T, (bs, 1)), GM_HEAD_DIM, axis=1)
        o_gm, vn = _sgu(a, row(gm_norm_v[i]), wbig, jnp.stack([bias_p, bias_s]).astype(F32),
                        rows=tm, n_prompt_blocks=n_prompt_blocks)

        ssd_params = (ssm_conv_w[i].astype(F32), row(ssm_conv_b[i]), _slab(ssm_dt_bias[i], DT_LANE0),
                      _slab(ssm_a_log[i], DT_LANE0), row(jnp.repeat(ssm_d[i], SSM_HEAD_DIM)), row(ssm_norm[i]),
                      tril3, e3)
        o_ssm_p, ssm_p, conv_p = _ssd_prompt(a, ssd_params, batch=bp, seq=tp,
                                             col_z=col_z, col_xbc=col_xbc, col_slab=col_slab)
        a_s = a[mp:]
        cs4 = jnp.pad(state_conv[i].astype(F32), ((0, 0), (0, ts - (SSM_CONV - 1)), (0, 0))).reshape(ms, conv_ch)
        o_ssm_s, ssm_s, conv_s = _ssd_sample(a_s, cs4, state_ssm, ssd_params, layer=i, steps=ts, nreq_blk=16,
                                             col_z=col_z, col_xbc=col_xbc, col_slab=col_slab)

        wuq = mla_w_uq[i]
        wuq_p = jnp.concatenate([wuq[:, :, :NOPE_DIM].reshape(q_rank, -1), wuq[:, :, NOPE_DIM:].reshape(q_rank, -1)], axis=1)
        mla_params = (row(mla_q_norm[i]), bf(wuq_p), row(mla_qn_norm[i]), row(jnp.tile(mla_qr_norm[i], LANES // ROPE_DIM)),
                      row(mla_kv_norm[i]), _slab(mla_kr_norm[i], 0), bd)
        qn, qr, c_lat, k_rope, kr2 = _mla_proj(a, cos_t, sin_s, mla_params, tm=tm,
                                               col_cq=col_cq, col_ckv=col_ckv, col_slab=col_slab, qscale=qscale)
        wuk = mla_w_uk[i]
        wuv = mla_w_uv[i]
        gkn = row(mla_kn_norm[i])
        wuk2 = bf(wuk.reshape(kv_rank, -1))
        k_p, v_p = _kv_proj(c_lat, kr2, wuk2, bf(wuv.reshape(kv_rank, -1)), gkn, rows=mp, tk=tm)
        o_mla_p = _flash(qn, qr, k_p, v_p, batch=bp, seq=tp, blk=tm)

        w_absorb = bf((wuk * mla_kn_norm[i][None, None, :]).transpose(1, 2, 0))
        qp = _absorb(qn, w_absorb, row0=mp, rows=ms)
        qp = qp.reshape(mla_heads, bs, ts, kv_rank).transpose(1, 0, 2, 3).reshape(bs, mla_heads * ts, kv_rank)
        qr_s = qr[mp:].astype(F32).reshape(bs, ts, mla_heads, ROPE_DIM).transpose(0, 2, 1, 3).reshape(bs, mla_heads * ts, ROPE_DIM)
        o_lat = _paged(page_table, qp, qr_s, c_lat[mp:].reshape(bs, ts, kv_rank), k_rope[mp:].reshape(bs, ts, LANES),
                       bf(wuk.transpose(0, 2, 1).reshape(kv_rank, -1)), ones8, cache_mla_latent, pool_rt,
                       layer=i, pages=min(PAGED_PAGES, page_table.shape[1]))
        o_lat = o_lat.reshape(bs, mla_heads, ts, kv_rank).transpose(1, 0, 2, 3).reshape(mla_heads, ms, kv_rank)
        o_mla_s = _v_up(o_lat, bf(wuv.transpose(1, 0, 2)))

        h = _out_proj(h, o_gm, o_ssm_p, o_ssm_s, o_mla_p, o_mla_s, _cast(w_out, layer=i, tr=CAST_ROWS // 4),
                      tm=tm, tn=1024)
        wgu = _cast_gate_up(w_ffn2_gate, w_ffn2_up, layer=i, tf=FFN_TF, tr=CAST_ROWS)
        h = _ffn(h, row(norm_ffn2[i]), wgu, _cast(w_ffn2_down, layer=i, tr=CAST_ROWS // 4), tm=tm)
        pe = jnp.concatenate([p_prompt[i].reshape(mp, -1), p_sample[i].reshape(ms, -1)], axis=0)
        h = _ple(h, row(norm_ple[i]), pe, _cast(w_ple_gate, layer=i, tr=CAST_ROWS // 4), bf(w_ple_proj[i]),
                 tm=tm, tn=1024)

        new = (c_lat[:mp].reshape(bp, tp, kv_rank), k_rope[:mp, :ROPE_DIM].reshape(bp, tp, ROPE_DIM),
               ssm_p, conv_p[:, SUBLANES - (SSM_CONV - 1):],
               c_lat[mp:].reshape(bs, ts, kv_rank), k_rope[mp:, :ROPE_DIM].reshape(bs, ts, ROPE_DIM),
               ssm_s, conv_s.reshape(bs, ts, conv_ch)[:, :SSM_CONV - 1], vn.reshape(bs, ts, gm_width))
        for lst, val in zip(outs, new):
            lst.append(val)

    return (h[:mp].reshape(bp, tp, d_model), h[mp:].reshape(bs, ts, d_model)) + tuple(jnp.stack(o) for o in outs)
```

```python
import functools

import numpy as np
import jax
import jax.numpy as jnp
from jax import lax
from jax.experimental import pallas as pl
from jax.experimental.pallas import tpu as pltpu

F32 = jnp.float32
BF16 = jnp.bfloat16

EPS = 1e-6
ROPE_BASE = 10000.0
PAST_LEN = 8192
PAGE_SIZE = 128

LANES = 128
SUBLANES = 8
MIB = 1024 * 1024

GM_HEAD_DIM = 128
GM_CHUNK = 128
SSM_HEAD_DIM = 64
SSM_GROUPS = 2
SSM_STATE = 128
SSM_CONV = 4
SSM_CHUNK = 128
NOPE_DIM = 128
ROPE_DIM = 64
V_DIM = 128
Q_BLOCK = 128

DT_LANE0 = ROPE_DIM


def _cparams(sem, vmem_mib):
    return pltpu.CompilerParams(dimension_semantics=sem, vmem_limit_bytes=int(vmem_mib * MIB))


def _nt(a, b):
    return lax.dot_general(a, b, (((1,), (1,)), ((), ())), preferred_element_type=F32)


def _dot(a, b):
    return jnp.dot(a, b, preferred_element_type=F32)


def _silu(x):
    return x * jax.nn.sigmoid(x)


def _softplus(x):
    return jnp.maximum(x, 0.0) + jnp.log1p(jnp.exp(-jnp.abs(x)))


def _gelu_tanh(x):
    return 0.5 * x * (1.0 + jnp.tanh(np.sqrt(2.0 / np.pi).astype(np.float32) * (x + 0.044715 * (x * x * x))))


def _rms(x, gain):
    return x * lax.rsqrt(jnp.mean(x * x, axis=-1, keepdims=True) + EPS) * gain


def _split3(x):
    hi = x.astype(BF16)
    r1 = x - hi.astype(F32)
    mid = r1.astype(BF16)
    lo = (r1 - mid.astype(F32)).astype(BF16)
    return hi, mid, lo


def _expand_heads(x, e3):
    return _dot(jnp.concatenate(_split3(x), axis=1), e3)


FFN_TF = 512
CAST_ROWS = 2048
FFN_DOWN_CHUNK = 1024
NORM_ROWS = 64


def _norm_rows(x_ref, g_ref, xn_ref, copy_ref=None):
    g = g_ref[...]
    for r in range(x_ref.shape[0] // NORM_ROWS):
        rows = slice(r * NORM_ROWS, (r + 1) * NORM_ROWS)
        x = x_ref[rows, :]
        xn_ref[rows, :] = _rms(x, g).astype(BF16)
        if copy_ref is not None:
            copy_ref[rows, :] = x


def _ffn_body(x_ref, g_ref, wgu_ref, wd_ref, o_ref, xn_ref):
    @pl.when(pl.program_id(1) == 0)
    def _():
        _norm_rows(x_ref, g_ref, xn_ref, o_ref)

    tf = wd_ref.shape[0]
    gu = _dot(xn_ref[...], wgu_ref[0])
    act = (_silu(gu[:, :tf]) * gu[:, tf:] * 0.5).astype(BF16)
    dc = FFN_DOWN_CHUNK
    for n in range(o_ref.shape[1] // dc):
        o_ref[:, n * dc:(n + 1) * dc] += _dot(act, wd_ref[:, n * dc:(n + 1) * dc])


def _ffn(h, g, wgu, wd, *, tm):
    m, d = h.shape
    nf, _, tf2 = wgu.shape
    tf = tf2 // 2
    vmem = (tm * d * 4 * 3 + tm * d * 2 + 3 * 2 * d * tf * 2 + 3 * tm * tf * 4 + 2 * tm * FFN_DOWN_CHUNK * 4) / MIB + 3
    return pl.pallas_call(
        _ffn_body,
        grid=(m // tm, nf),
        in_specs=[
            pl.BlockSpec((tm, d), lambda i, j: (i, 0), pipeline_mode=pl.Buffered(1)),
            pl.BlockSpec((1, d), lambda i, j: (0, 0)),
            pl.BlockSpec((1, d, tf2), lambda i, j: (j, 0, 0)),
            pl.BlockSpec((tf, d), lambda i, j: (j, 0)),
        ],
        out_specs=pl.BlockSpec((tm, d), lambda i, j: (i, 0)),
        out_shape=jax.ShapeDtypeStruct((m, d), F32),
        scratch_shapes=[pltpu.VMEM((tm, d), BF16)],
        compiler_params=_cparams(("parallel", "arbitrary"), vmem),
        name="ffn",
    )(h, g, wgu, wd)


def _cast_gate_up_body(wg_ref, wu_ref, o_ref):
    tf = wg_ref.shape[2]
    o_ref[0, :, :tf] = wg_ref[0].astype(BF16)
    o_ref[0, :, tf:] = wu_ref[0].astype(BF16)


def _cast_gate_up(wg_all, wu_all, *, layer, tf, tr):
    _, d, f = wg_all.shape
    spec = pl.BlockSpec((1, tr, tf), lambda j, r: (layer, r, j))
    return pl.pallas_call(
        _cast_gate_up_body,
        grid=(f // tf, d // tr),
        in_specs=[spec, spec],
        out_specs=pl.BlockSpec((1, tr, 2 * tf), lambda j, r: (j, r, 0)),
        out_shape=jax.ShapeDtypeStruct((f // tf, d, 2 * tf), BF16),
        compiler_params=_cparams(("parallel", "parallel"), (2 * 2 * tr * tf * 4 + 2 * tr * 2 * tf * 2) / MIB + 4),
        name="cast_gate_up",
    )(wg_all, wu_all)


def _cast_body(w_ref, o_ref):
    o_ref[...] = w_ref[0].astype(BF16)


def _cast(w_all, *, layer, tr):
    _, r, c = w_all.shape
    return pl.pallas_call(
        _cast_body,
        grid=(r // tr,),
        in_specs=[pl.BlockSpec((1, tr, c), lambda i: (layer, i, 0))],
        out_specs=pl.BlockSpec((tr, c), lambda i: (i, 0)),
        out_shape=jax.ShapeDtypeStruct((r, c), BF16),
        compiler_params=_cparams(("parallel",), (2 * tr * c * 4 + 2 * tr * c * 2) / MIB + 4),
        name="cast",
    )(w_all)


def _cast_w_in_body(w_ref, o_ref, *, dt0, n_dt):
    x = w_ref[0]
    n = x.shape[1]
    tail = n - dt0 - n_dt
    o_ref[:, :dt0] = x[:, :dt0].astype(BF16)
    o_ref[:, dt0:dt0 + tail] = x[:, dt0 + n_dt:].astype(BF16)
    o_ref[:, dt0 + tail:n] = x[:, dt0:dt0 + n_dt].astype(BF16)
    o_ref[:, n:] = jnp.zeros((x.shape[0], o_ref.shape[1] - n), BF16)


def _cast_w_in(w_all, *, layer, dt0, n_dt, n_out, tr):
    _, r, c = w_all.shape
    return pl.pallas_call(
        functools.partial(_cast_w_in_body, dt0=dt0, n_dt=n_dt),
        grid=(r // tr,),
        in_specs=[pl.BlockSpec((1, tr, c), lambda i: (layer, i, 0))],
        out_specs=pl.BlockSpec((tr, n_out), lambda i: (i, 0)),
        out_shape=jax.ShapeDtypeStruct((r, n_out), BF16),
        compiler_params=_cparams(("parallel",), (3 * tr * c * 4 + 3 * tr * n_out * 2) / MIB + 4),
        name="cast_w_in",
    )(w_all)


def _norm_mm_body(x_ref, g_ref, w_ref, o_ref, xn_ref):
    @pl.when(pl.program_id(1) == 0)
    def _():
        _norm_rows(x_ref, g_ref, xn_ref)

    o_ref[...] = _dot(xn_ref[...], w_ref[...])


def _norm_matmul(h, g, w, *, tm, tn):
    m, d = h.shape
    n = w.shape[1]
    vmem = (2 * tm * d * 4 + tm * d * 2 + 2 * d * tn * 2 + 3 * tm * tn * 4) / MIB + 4
    return pl.pallas_call(
        _norm_mm_body,
        grid=(m // tm, n // tn),
        in_specs=[
            pl.BlockSpec((tm, d), lambda i, j: (i, 0), pipeline_mode=pl.Buffered(1)),
            pl.BlockSpec((1, d), lambda i, j: (0, 0)),
            pl.BlockSpec((d, tn), lambda i, j: (0, j)),
        ],
        out_specs=pl.BlockSpec((tm, tn), lambda i, j: (i, j)),
        out_shape=jax.ShapeDtypeStruct((m, n), F32),
        scratch_shapes=[pltpu.VMEM((tm, d), BF16)],
        compiler_params=_cparams(("parallel", "arbitrary"), vmem),
        name="in_proj",
    )(h, g, w)


def _out_proj_body(gm_ref, ssm_p_ref, ssm_s_ref, mla_p_ref, mla_s_ref, w_ref, res_ref, o_ref, *, n_prompt_blocks):
    is_sample = pl.program_id(0) >= n_prompt_blocks
    ssm = jnp.where(is_sample, ssm_s_ref[...], ssm_p_ref[...])
    mla = jnp.where(is_sample, mla_s_ref[...], mla_p_ref[...])
    k1 = gm_ref.shape[1]
    k2 = k1 + ssm.shape[1]
    o_ref[...] = (res_ref[...] + _dot(gm_ref[...], w_ref[0:k1, :]) + _dot(ssm, w_ref[k1:k2, :])
                  + _dot(mla, w_ref[k2:, :]))


def _out_proj(h, gm, ssm_p, ssm_s, mla_p, mla_s, w, *, tm, tn):
    m, d = h.shape
    k = w.shape[0]
    npb = ssm_p.shape[0] // tm
    assert ssm_s.shape[0] == tm and mla_s.shape[0] == tm
    vmem = (2 * 2 * tm * k * 2 + 2 * k * tn * 2 + 6 * tm * tn * 4) / MIB + 4
    prompt = lambda width: pl.BlockSpec((tm, width), lambda i, j: (jnp.minimum(i, npb - 1), 0))
    sample = lambda width: pl.BlockSpec((tm, width), lambda i, j: (0, 0))
    return pl.pallas_call(
        functools.partial(_out_proj_body, n_prompt_blocks=npb),
        grid=(m // tm, d // tn),
        in_specs=[
            pl.BlockSpec((tm, gm.shape[1]), lambda i, j: (i, 0)),
            prompt(ssm_p.shape[1]), sample(ssm_s.shape[1]), prompt(mla_p.shape[1]), sample(mla_s.shape[1]),
            pl.BlockSpec((k, tn), lambda i, j: (0, j)),
            pl.BlockSpec((tm, tn), lambda i, j: (i, j)),
        ],
        out_specs=pl.BlockSpec((tm, tn), lambda i, j: (i, j)),
        out_shape=jax.ShapeDtypeStruct((m, d), F32),
        compiler_params=_cparams(("parallel", "arbitrary"), vmem),
        name="out_proj",
    )(gm, ssm_p, ssm_s, mla_p, mla_s, w, h)


def _ple_body(x_ref, g_ref, pe_ref, wg_ref, wp_ref, res_ref, o_ref, xn_ref):
    @pl.when(pl.program_id(1) == 0)
    def _():
        _norm_rows(x_ref, g_ref, xn_ref)

    gate = jax.nn.sigmoid(_dot(xn_ref[...], wg_ref[...]))
    proj = _dot(pe_ref[...].astype(BF16), wp_ref[...])
    o_ref[...] = res_ref[...] + gate * proj


def _ple(h, g, pe, wg, wp, *, tm, tn):
    m, d = h.shape
    dp = pe.shape[1]
    vmem = (2 * tm * d * 4 + tm * d * 2 + 2 * tm * dp * 4 + 2 * d * tn * 2 + 2 * dp * tn * 2 + 8 * tm * tn * 4) / MIB + 4
    return pl.pallas_call(
        _ple_body,
        grid=(m // tm, d // tn),
        in_specs=[
            pl.BlockSpec((tm, d), lambda i, j: (i, 0), pipeline_mode=pl.Buffered(1)),
            pl.BlockSpec((1, d), lambda i, j: (0, 0)),
            pl.BlockSpec((tm, dp), lambda i, j: (i, 0)),
            pl.BlockSpec((d, tn), lambda i, j: (0, j)),
            pl.BlockSpec((dp, tn), lambda i, j: (0, j)),
            pl.BlockSpec((tm, tn), lambda i, j: (i, j)),
        ],
        out_specs=pl.BlockSpec((tm, tn), lambda i, j: (i, j)),
        out_shape=jax.ShapeDtypeStruct((m, d), F32),
        scratch_shapes=[pltpu.VMEM((tm, d), BF16)],
        compiler_params=_cparams(("parallel", "arbitrary"), vmem),
        name="ple",
    )(h, g, pe, wg, wp, h)


def _sgu_body(u_ref, v_ref, gain_ref, w_ref, bias_ref, o_ref, vn_ref, *, heads):
    u = _gelu_tanh(u_ref[...])
    v = _gelu_tanh(v_ref[...])
    for g in range(heads):
        sl = slice(g * GM_HEAD_DIM, (g + 1) * GM_HEAD_DIM)
        vn = _rms(v[:, sl], gain_ref[:, sl])
        vn_ref[:, sl] = vn
        mixed = _dot(w_ref[0, g], vn.astype(BF16)) + bias_ref[0, :, sl]
        o_ref[:, sl] = (u[:, sl] * mixed).astype(o_ref.dtype)


def _sgu(a, gain, wbig, bias, *, rows, n_prompt_blocks):
    m = a.shape[0]
    width = gain.shape[1]
    heads = width // GM_HEAD_DIM
    sel = lambda i: jnp.minimum(i // n_prompt_blocks, 1)
    return pl.pallas_call(
        functools.partial(_sgu_body, heads=heads),
        grid=(m // rows,),
        in_specs=[
            pl.BlockSpec((rows, width), lambda i: (i, 0)),
            pl.BlockSpec((rows, width), lambda i: (i, 1)),
            pl.BlockSpec((1, width), lambda i: (0, 0)),
            pl.BlockSpec((1, heads, rows, rows), lambda i: (sel(i), 0, 0, 0)),
            pl.BlockSpec((1, rows, width), lambda i: (sel(i), 0, 0)),
        ],
        out_specs=[
            pl.BlockSpec((rows, width), lambda i: (i, 0)),
            pl.BlockSpec((rows, width), lambda i: (0, 0)),
        ],
        out_shape=[jax.ShapeDtypeStruct((m, width), BF16), jax.ShapeDtypeStruct((rows, width), F32)],
        compiler_params=_cparams(("arbitrary",), 40),
        name="sgu",
    )(a, a, gain, wbig, bias)


def _ssd_common_tail(y, xs, z, dfull, norm, o_ref, width):
    y = y + xs * dfull
    gt = y * _silu(z)
    gw = width // SSM_GROUPS
    for g in range(SSM_GROUPS):
        sl = slice(g * gw, (g + 1) * gw)
        seg = gt[:, sl]
        o_ref[:, sl] = (seg * lax.rsqrt(jnp.mean(seg * seg, axis=-1, keepdims=True) + EPS) * norm[:, sl]).astype(o_ref.dtype)


def _dt_slab(slab_ref, dtb_ref, alog_ref, heads):
    lane = lax.broadcasted_iota(jnp.int32, slab_ref.shape, 1)
    raw = jnp.where((lane >= DT_LANE0) & (lane < DT_LANE0 + heads), slab_ref[...], 0.0)
    dt = _softplus(raw + dtb_ref[...])
    return dt, dt * (-jnp.exp(alog_ref[...]))


def _ssd_prompt_body(z_ref, xbc_ref, slab_ref, cw_ref, cb_ref, dtb_ref, alog_ref, dfull_ref, norm_ref, tril3_ref,
                     e3_ref, o_ref, st_out_ref, conv_out_ref, full_ref, state_ref, y_ref, *, chunk, n_chunks, heads):
    c = pl.program_id(1)
    width = heads * SSM_HEAD_DIM
    hpg = heads // SSM_GROUPS
    gw = hpg * SSM_HEAD_DIM
    tail = SUBLANES

    @pl.when(c == 0)
    def _():
        full_ref[0:tail, :] = jnp.zeros((tail, full_ref.shape[1]), F32)
        state_ref[...] = jnp.zeros(state_ref.shape, F32)

    full_ref[tail:tail + chunk, :] = xbc_ref[...]
    conv = cb_ref[...]
    for k in range(SSM_CONV):
        conv = conv + cw_ref[k:k + 1, :] * full_ref[pl.ds(tail - (SSM_CONV - 1) + k, chunk), :]
    full_ref[0:tail, :] = full_ref[chunk:chunk + tail, :]
    xc = _silu(conv)
    xs = xc[:, :width]
    bm = xc[:, width:width + SSM_GROUPS * SSM_STATE].astype(BF16)
    cm = xc[:, width + SSM_GROUPS * SSM_STATE:].astype(BF16)

    dt, a = _dt_slab(slab_ref, dtb_ref, alog_ref, heads)
    acs = _dot(tril3_ref[...], jnp.concatenate(_split3(a), axis=0))
    acs_t = acs.T
    arem = acs[chunk - 1:chunk, :] - acs
    e3 = e3_ref[...]
    xdt = xs * _expand_heads(dt, e3)
    ecs = jnp.exp(_expand_heads(acs, e3))
    xw_t = (xdt * jnp.exp(_expand_heads(arem, e3))).T.astype(BF16)
    ecol = ecs.T[:, chunk - 1:chunk]
    xdt_b = xdt.astype(BF16)

    row_i = lax.broadcasted_iota(jnp.int32, (chunk, chunk), 0)
    col_i = lax.broadcasted_iota(jnp.int32, (chunk, chunk), 1)
    causal = row_i >= col_i
    for g in range(SSM_GROUPS):
        cg = cm[:, g * SSM_STATE:(g + 1) * SSM_STATE]
        bg = bm[:, g * SSM_STATE:(g + 1) * SSM_STATE]
        cb = _nt(cg, bg)
        st = state_ref[g * hpg:(g + 1) * hpg].reshape(gw, SSM_STATE)
        y_ref[:, g * gw:(g + 1) * gw] = _nt(cg, st.astype(BF16)) * ecs[:, g * gw:(g + 1) * gw]
        for hh in range(hpg):
            h = g * hpg + hh
            col = acs[:, DT_LANE0 + h:DT_LANE0 + h + 1]
            row = acs_t[DT_LANE0 + h:DT_LANE0 + h + 1, :]
            mat = jnp.where(causal, cb * jnp.exp(col - row), 0.0).astype(BF16)
            hs = slice(h * SSM_HEAD_DIM, (h + 1) * SSM_HEAD_DIM)
            y_ref[:, hs] += _dot(mat, xdt_b[:, hs])
        s_new = _dot(xw_t[g * gw:(g + 1) * gw, :], bg)
        state_ref[g * hpg:(g + 1) * hpg] = (st * ecol[g * gw:(g + 1) * gw, :] + s_new).reshape(hpg, SSM_HEAD_DIM, SSM_STATE)

    _ssd_common_tail(y_ref[...], xs, z_ref[...], dfull_ref[...], norm_ref[...], o_ref, width)

    @pl.when(c == n_chunks - 1)
    def _():
        st_out_ref[0] = state_ref[...]
        conv_out_ref[0] = full_ref[0:tail, :]


def _ssd_prompt(a, params, *, batch, seq, col_z, col_xbc, col_slab):
    cw, cb, dtb, alog, dfull, norm, tril3, e3 = params
    width = norm.shape[1]
    heads = width // SSM_HEAD_DIM
    conv_ch = cw.shape[1]
    chunk = SSM_CHUNK
    nc = seq // chunk
    const = lambda shape: pl.BlockSpec(shape, lambda b, c: (0,) * len(shape))
    rowblk = lambda b, c: b * nc + c
    return pl.pallas_call(
        functools.partial(_ssd_prompt_body, chunk=chunk, n_chunks=nc, heads=heads),
        grid=(batch, nc),
        in_specs=[
            pl.BlockSpec((chunk, width), lambda b, c: (rowblk(b, c), col_z // width)),
            pl.BlockSpec((chunk, conv_ch), lambda b, c: (rowblk(b, c), col_xbc // conv_ch)),
            pl.BlockSpec((chunk, LANES), lambda b, c: (rowblk(b, c), col_slab // LANES)),
            const(cw.shape), const(cb.shape), const(dtb.shape), const(alog.shape), const(dfull.shape),
            const(norm.shape), const(tril3.shape), const(e3.shape),
        ],
        out_specs=[
            pl.BlockSpec((chunk, width), lambda b, c: (rowblk(b, c), 0)),
            pl.BlockSpec((1, heads, SSM_HEAD_DIM, SSM_STATE), lambda b, c: (b, 0, 0, 0)),
            pl.BlockSpec((1, SUBLANES, conv_ch), lambda b, c: (b, 0, 0)),
        ],
        out_shape=[
            jax.ShapeDtypeStruct((batch * seq, width), BF16),
            jax.ShapeDtypeStruct((batch, heads, SSM_HEAD_DIM, SSM_STATE), F32),
            jax.ShapeDtypeStruct((batch, SUBLANES, conv_ch), F32),
        ],
        scratch_shapes=[
            pltpu.VMEM((chunk + SUBLANES, conv_ch), F32),
            pltpu.VMEM((heads, SSM_HEAD_DIM, SSM_STATE), F32),
            pltpu.VMEM((chunk, width), F32),
        ],
        compiler_params=_cparams(("parallel", "arbitrary"), 40),
        name="ssd_prompt",
    )(a, a, a, cw, cb, dtb, alog, dfull, norm, tril3, e3)


def _ssd_sample_body(z_ref, xbc_ref, slab_ref, cs_ref, st_ref, cw_ref, cb_ref, dtb_ref, alog_ref, dfull_ref, norm_ref,
                     e3_ref, o_ref, st_out_ref, conv_out_ref, *, rows, steps, heads):
    width = heads * SSM_HEAD_DIM
    hpg = heads // SSM_GROUPS
    gw = hpg * SSM_HEAD_DIM
    nreq = rows // steps
    x = xbc_ref[...]
    cs = cs_ref[...]
    t_of = lambda shape: lax.broadcasted_iota(jnp.int32, shape, 0) % steps

    def back(v, j):
        return v if j == 0 else pltpu.roll(v, j, axis=0)

    def fwd(v, j):
        return v if j == 0 else pltpu.roll(v, rows - j, axis=0)

    tx = t_of(x.shape)
    conv = cb_ref[...] + cw_ref[SSM_CONV - 1:SSM_CONV, :] * x
    for k in range(SSM_CONV - 1):
        j = SSM_CONV - 1 - k
        conv = conv + cw_ref[k:k + 1, :] * jnp.where(tx >= j, back(x, j), fwd(cs, k))
    conv_out_ref[...] = fwd(x, steps - (SSM_CONV - 1))
    xc = _silu(conv)
    xs = xc[:, :width]
    bm = xc[:, width:width + SSM_GROUPS * SSM_STATE]
    cm = xc[:, width + SSM_GROUPS * SSM_STATE:]

    dt, a = _dt_slab(slab_ref, dtb_ref, alog_ref, heads)
    ts = t_of(a.shape)
    acs = a
    arem = jnp.zeros_like(a)
    for j in range(1, steps):
        acs = acs + jnp.where(ts >= j, back(a, j), 0.0)
        arem = arem + jnp.where(ts < steps - j, fwd(a, j), 0.0)
    e3 = e3_ref[...]
    acsx = _expand_heads(acs, e3)
    xdt = xs * _expand_heads(dt, e3)
    ecs = jnp.exp(acsx)
    xw = xdt * jnp.exp(_expand_heads(arem, e3))

    tw = t_of(xs.shape)
    y = jnp.zeros_like(xs)
    for j in range(steps):
        bj = back(bm, j)
        dec = jnp.exp(acsx - back(acsx, j)) * back(xdt, j)
        parts = []
        for g in range(SSM_GROUPS):
            sl = slice(g * SSM_STATE, (g + 1) * SSM_STATE)
            cbj = jnp.sum(cm[:, sl] * bj[:, sl], axis=-1, keepdims=True)
            parts.append(cbj * dec[:, g * gw:(g + 1) * gw])
        y = y + jnp.where(tw >= j, jnp.concatenate(parts, axis=1), 0.0)

    pad = jnp.zeros((LANES - rows, width), F32)
    xw_t = jnp.concatenate([xw, pad], axis=0).T.astype(BF16)
    e_t = jnp.concatenate([ecs, pad], axis=0).T
    cm_b = cm.astype(BF16)
    bm_pad = jnp.concatenate([bm, jnp.zeros((LANES - rows, bm.shape[1]), F32)], axis=0)
    req_y = lax.broadcasted_iota(jnp.int32, (rows, gw), 0) // steps
    req_b = lax.broadcasted_iota(jnp.int32, (LANES, SSM_STATE), 0) // steps
    yoff = [jnp.zeros((rows, gw), F32) for _ in range(SSM_GROUPS)]
    for b in range(nreq):
        last = b * steps + steps - 1
        for g in range(SSM_GROUPS):
            sl = slice(g * SSM_STATE, (g + 1) * SSM_STATE)
            h0 = st_ref[0, b, g * hpg:(g + 1) * hpg].reshape(gw, SSM_STATE)
            yoff[g] = yoff[g] + jnp.where(req_y == b, _nt(cm_b[:, sl], h0.astype(BF16)), 0.0)
            b_only = jnp.where(req_b == b, bm_pad[:, sl], 0.0).astype(BF16)
            s_new = _dot(xw_t[g * gw:(g + 1) * gw, :], b_only)
            decay = e_t[g * gw:(g + 1) * gw, last:last + 1]
            st_out_ref[0, b, g * hpg:(g + 1) * hpg] = (h0 * decay + s_new).reshape(hpg, SSM_HEAD_DIM, SSM_STATE)
    y = y + jnp.concatenate(yoff, axis=1) * ecs

    _ssd_common_tail(y, xs, z_ref[...], dfull_ref[...], norm_ref[...], o_ref, width)


def _ssd_sample(a_s, cs4, state, params, *, layer, steps, nreq_blk, col_z, col_xbc, col_slab):
    cw, cb, dtb, alog, dfull, norm, _, e3 = params
    width = norm.shape[1]
    heads = width // SSM_HEAD_DIM
    conv_ch = cw.shape[1]
    m = a_s.shape[0]
    rows = nreq_blk * steps
    const = lambda shape: pl.BlockSpec(shape, lambda i: (0,) * len(shape))
    st_spec = pl.BlockSpec((1, nreq_blk, heads, SSM_HEAD_DIM, SSM_STATE), lambda i: (layer, i, 0, 0, 0))
    return pl.pallas_call(
        functools.partial(_ssd_sample_body, rows=rows, steps=steps, heads=heads),
        grid=(m // rows,),
        in_specs=[
            pl.BlockSpec((rows, width), lambda i: (i, col_z // width)),
            pl.BlockSpec((rows, conv_ch), lambda i: (i, col_xbc // conv_ch)),
            pl.BlockSpec((rows, LANES), lambda i: (i, col_slab // LANES)),
            pl.BlockSpec((rows, conv_ch), lambda i: (i, 0)),
            st_spec,
            const(cw.shape), const(cb.shape), const(dtb.shape), const(alog.shape), const(dfull.shape),
            const(norm.shape), const(e3.shape),
        ],
        out_specs=[
            pl.BlockSpec((rows, width), lambda i: (i, 0)),
            st_spec,
            pl.BlockSpec((rows, conv_ch), lambda i: (i, 0)),
        ],
        out_shape=[
            jax.ShapeDtypeStruct((m, width), BF16),
            jax.ShapeDtypeStruct(state.shape, F32),
            jax.ShapeDtypeStruct((m, conv_ch), F32),
        ],
        input_output_aliases={4: 1},
        compiler_params=_cparams(("parallel",), 48),
        name="ssd_sample",
    )(a_s, a_s, a_s, cs4, state, cw, cb, dtb, alog, dfull, norm, e3)


def _rms_pairs(x, bd):
    s = x * x
    hi = s.astype(BF16)
    lo = (s - hi.astype(F32)).astype(BF16)
    ssq = _dot(hi, bd) + _dot(lo, bd)
    return x * lax.rsqrt(ssq * (1.0 / ROPE_DIM) + EPS)


def _rope_pairs(y, cos_t, sin_s):
    lane = lax.broadcasted_iota(jnp.int32, y.shape, 1)
    half = ROPE_DIM // 2
    rot = jnp.where((lane % ROPE_DIM) < half, pltpu.roll(y, LANES - half, axis=1), pltpu.roll(y, half, axis=1))
    return y * cos_t + rot * sin_s


def _mla_proj_body(cq_ref, ckv_ref, slab_ref, cos_ref, sin_ref, gq_ref, wuq_ref, gqn_ref, gqr_ref, gkv_ref, gkr_ref,
                   bd_ref, qn_ref, qr_ref, c_ref, kr_ref, kr2_ref, *, heads, qscale):
    cqn = _rms(cq_ref[...], gq_ref[...]).astype(BF16)
    q = _dot(cqn, wuq_ref[...])
    gqn = gqn_ref[...]
    for h in range(heads):
        sl = slice(h * NOPE_DIM, (h + 1) * NOPE_DIM)
        qn_ref[:, sl] = (_rms(q[:, sl], gqn) * qscale).astype(BF16)
    bd = bd_ref[...]
    cos_t = cos_ref[...]
    sin_s = sin_ref[...]
    base = heads * NOPE_DIM
    for j in range(heads * ROPE_DIM // LANES):
        x = q[:, base + j * LANES: base + (j + 1) * LANES]
        y = _rope_pairs(_rms_pairs(x, bd) * gqr_ref[...], cos_t, sin_s)
        qr_ref[:, j * LANES:(j + 1) * LANES] = (y * qscale).astype(BF16)
    c_ref[...] = _rms(ckv_ref[...], gkv_ref[...])
    kr = _rope_pairs(_rms_pairs(slab_ref[...], bd) * gkr_ref[...], cos_t, sin_s)
    kr_ref[...] = kr
    kr2_ref[...] = (kr + pltpu.roll(kr, ROPE_DIM, axis=1)).astype(BF16)


def _mla_proj(a, cos_t, sin_s, params, *, tm, col_cq, col_ckv, col_slab, qscale):
    gq, wuq, gqn, gqr, gkv, gkr, bd = params
    m = a.shape[0]
    q_rank = gq.shape[1]
    kv_rank = gkv.shape[1]
    heads = wuq.shape[1] // (NOPE_DIM + ROPE_DIM)
    const = lambda shape: pl.BlockSpec(shape, lambda i: (0,) * len(shape))
    row = lambda w: pl.BlockSpec((tm, w), lambda i: (i, 0))
    return pl.pallas_call(
        functools.partial(_mla_proj_body, heads=heads, qscale=qscale),
        grid=(m // tm,),
        in_specs=[
            pl.BlockSpec((tm, q_rank), lambda i: (i, col_cq // q_rank)),
            pl.BlockSpec((tm, kv_rank), lambda i: (i, col_ckv // kv_rank)),
            pl.BlockSpec((tm, LANES), lambda i: (i, col_slab // LANES)),
            row(LANES), row(LANES),
            const(gq.shape), const(wuq.shape), const(gqn.shape), const(gqr.shape), const(gkv.shape),
            const(gkr.shape), const(bd.shape),
        ],
        out_specs=[row(heads * NOPE_DIM), row(heads * ROPE_DIM), row(kv_rank), row(LANES), row(LANES)],
        out_shape=[
            jax.ShapeDtypeStruct((m, heads * NOPE_DIM), BF16),
            jax.ShapeDtypeStruct((m, heads * ROPE_DIM), BF16),
            jax.ShapeDtypeStruct((m, kv_rank), F32),
            jax.ShapeDtypeStruct((m, LANES), F32),
            jax.ShapeDtypeStruct((m, LANES), BF16),
        ],
        compiler_params=_cparams(("parallel",), 48),
        name="mla_proj",
    )(a, a, a, cos_t, sin_s, gq, wuq, gqn, gqr, gkv, gkr, bd)


def _kv_proj_body(c_ref, kr2_ref, wuk_ref, wuv_ref, gkn_ref, k_ref, v_ref, *, heads):
    cb = c_ref[...].astype(BF16)
    k = _dot(cb, wuk_ref[...])
    gkn = gkn_ref[...]
    kr2 = kr2_ref[...]
    for h in range(heads):
        sl = slice(h * NOPE_DIM, (h + 1) * NOPE_DIM)
        k_ref[:, 2 * h * NOPE_DIM:(2 * h + 1) * NOPE_DIM] = _rms(k[:, sl], gkn).astype(BF16)
        k_ref[:, (2 * h + 1) * NOPE_DIM:(2 * h + 2) * NOPE_DIM] = kr2
    v_ref[...] = _dot(cb, wuv_ref[...]).astype(BF16)


def _kv_proj(c, kr2, wuk, wuv, gkn, *, rows, tk):
    kv_rank = c.shape[1]
    n = wuk.shape[1]
    heads = n // NOPE_DIM
    const = lambda shape: pl.BlockSpec(shape, lambda i: (0,) * len(shape))
    return pl.pallas_call(
        functools.partial(_kv_proj_body, heads=heads),
        grid=(rows // tk,),
        in_specs=[pl.BlockSpec((tk, kv_rank), lambda i: (i, 0)), pl.BlockSpec((tk, LANES), lambda i: (i, 0)),
                  const(wuk.shape), const(wuv.shape), const(gkn.shape)],
        out_specs=[pl.BlockSpec((tk, 2 * n), lambda i: (i, 0)), pl.BlockSpec((tk, n), lambda i: (i, 0))],
        out_shape=[jax.ShapeDtypeStruct((rows, 2 * n), BF16), jax.ShapeDtypeStruct((rows, n), BF16)],
        compiler_params=_cparams(("parallel",), 32),
        name="kv_proj",
    )(c, kr2, wuk, wuv, gkn)


FLASH_HEADS = 2


def _flash_body(qn_ref, qr_ref, k_ref, v_ref, o_ref, m_ref, l_ref, acc_ref, *, blk):
    i = pl.program_id(2)
    lane = lax.broadcasted_iota(jnp.int32, qr_ref.shape, 1)
    zero = jnp.zeros_like(qr_ref[...])
    q = []
    for hh in range(FLASH_HEADS):
        qr = jnp.where((lane // ROPE_DIM) == hh, qr_ref[...], zero)
        q.append(jnp.concatenate([qn_ref[:, hh * NOPE_DIM:(hh + 1) * NOPE_DIM], qr], axis=1))
    m_ref[...] = jnp.full(m_ref.shape, -jnp.inf, F32)
    l_ref[...] = jnp.zeros(l_ref.shape, F32)
    acc_ref[...] = jnp.zeros(acc_ref.shape, F32)
    kw = NOPE_DIM + LANES
    reps = blk // LANES

    def step(j, masked):
        rows = pl.ds(pl.multiple_of(j * blk, blk), blk)
        for hh in range(FLASH_HEADS):
            s = _nt(q[hh], k_ref[rows, hh * kw:(hh + 1) * kw])
            if masked:
                r = lax.broadcasted_iota(jnp.int32, s.shape, 0)
                c = lax.broadcasted_iota(jnp.int32, s.shape, 1)
                s = jnp.where(c <= r, s, -jnp.inf)
            m_old = m_ref[hh]
            m_new = jnp.maximum(m_old, jnp.max(s, axis=-1, keepdims=True))
            corr = jnp.exp2(m_old - m_new)
            p = jnp.exp2(s - jnp.tile(m_new, (1, reps)))
            l_ref[hh] = l_ref[hh] * corr + jnp.sum(p, axis=-1, keepdims=True)
            acc_ref[hh] = acc_ref[hh] * corr + _dot(p.astype(BF16), v_ref[rows, hh * V_DIM:(hh + 1) * V_DIM])
            m_ref[hh] = m_new

    def body(j, carry):
        step(j, False)
        return carry

    lax.fori_loop(0, i, body, 0)
    step(i, True)
    for hh in range(FLASH_HEADS):
        o_ref[:, hh * V_DIM:(hh + 1) * V_DIM] = (acc_ref[hh] / l_ref[hh]).astype(o_ref.dtype)


def _flash(qn, qr, kcat, v, *, batch, seq, blk):
    heads = v.shape[1] // V_DIM
    nb = seq // blk
    hp = FLASH_HEADS
    return pl.pallas_call(
        functools.partial(_flash_body, blk=blk),
        grid=(batch, heads // hp, nb),
        in_specs=[
            pl.BlockSpec((blk, hp * NOPE_DIM), lambda b, h, i: (b * nb + i, h)),
            pl.BlockSpec((blk, LANES), lambda b, h, i: (b * nb + i, h)),
            pl.BlockSpec((seq, hp * (NOPE_DIM + LANES)), lambda b, h, i: (b, h)),
            pl.BlockSpec((seq, hp * V_DIM), lambda b, h, i: (b, h)),
        ],
        out_specs=pl.BlockSpec((blk, hp * V_DIM), lambda b, h, i: (b * nb + i, h)),
        out_shape=jax.ShapeDtypeStruct((batch * seq, heads * V_DIM), BF16),
        scratch_shapes=[pltpu.VMEM((hp, blk, LANES), F32), pltpu.VMEM((hp, blk, LANES), F32),
                        pltpu.VMEM((hp, blk, V_DIM), F32)],
        compiler_params=_cparams(("parallel", "parallel", "arbitrary"), 32),
        name="flash",
    )(qn, qr, kcat, v)


def _absorb_body(qn_ref, w_ref, o_ref):
    o_ref[0] = _dot(qn_ref[...], w_ref[0]).astype(o_ref.dtype)


def _absorb(qn, w, *, row0, rows):
    heads, _, kv_rank = w.shape
    return pl.pallas_call(
        _absorb_body,
        grid=(heads,),
        in_specs=[
            pl.BlockSpec((rows, NOPE_DIM), lambda h: (row0 // rows, h)),
            pl.BlockSpec((1, NOPE_DIM, kv_rank), lambda h: (h, 0, 0)),
        ],
        out_specs=pl.BlockSpec((1, rows, kv_rank), lambda h: (h, 0, 0)),
        out_shape=jax.ShapeDtypeStruct((heads, rows, kv_rank), F32),
        compiler_params=_cparams(("parallel",), 16),
        name="absorb",
    )(qn, w)


def _v_up_body(o_ref_in, w_ref, o_ref):
    o_ref[...] = _dot(o_ref_in[0].astype(BF16), w_ref[0]).astype(o_ref.dtype)


def _v_up(o_lat, w):
    heads, rows, kv_rank = o_lat.shape
    return pl.pallas_call(
        _v_up_body,
        grid=(heads,),
        in_specs=[
            pl.BlockSpec((1, rows, kv_rank), lambda h: (h, 0, 0)),
            pl.BlockSpec((1, kv_rank, V_DIM), lambda h: (h, 0, 0)),
        ],
        out_specs=pl.BlockSpec((rows, V_DIM), lambda h: (0, h)),
        out_shape=jax.ShapeDtypeStruct((rows, heads * V_DIM), BF16),
        compiler_params=_cparams(("parallel",), 16),
        name="v_up",
    )(o_lat, w)


KEY_CHUNK = 512
PAGED_SPLITS = 1
PAGED_PAGES = 32


def _paged_body(pt_ref, qp_ref, qr_ref, cnew_ref, krnew_ref, wuk_ref, ones_ref, *rest, pages, heads, steps):
    c_refs = rest[:pages]
    krt_refs = rest[pages:2 * pages]
    o_ref, m_ref, l_ref, acc_ref, cpad_ref, krpad_ref = rest[2 * pages:]
    j = pl.program_id(1)
    nj = pl.num_programs(1)
    qrows = heads * steps
    qp = qp_ref[0].astype(BF16)
    qr = qr_ref[0].astype(BF16)

    @pl.when(j == 0)
    def _():
        m_ref[...] = jnp.full(m_ref.shape, -jnp.inf, F32)
        l_ref[...] = jnp.zeros(l_ref.shape, F32)
        acc_ref[...] = jnp.zeros(acc_ref.shape, F32)

    def attend(c_blk, krt_blk, valid):
        folded = None
        for t in range(wuk_ref.shape[1] // KEY_CHUNK):
            k = _dot(c_blk, wuk_ref[:, t * KEY_CHUNK:(t + 1) * KEY_CHUNK])
            sq = k * k
            for u in range(KEY_CHUNK // LANES):
                tile = sq[:, u * LANES:(u + 1) * LANES]
                folded = tile if folded is None else folded + tile
        hi = folded.astype(BF16)
        lo = (folded - hi.astype(F32)).astype(BF16)
        ssq = _dot(hi, ones_ref[...]) + _dot(lo, ones_ref[...])
        rinv_t = lax.rsqrt(ssq * (1.0 / NOPE_DIM) + EPS).T[:qrows, :]
        s = _nt(qp, c_blk) * rinv_t + _dot(qr, krt_blk)
        if valid is not None:
            s = jnp.where(valid, s, -jnp.inf)
        m_old = m_ref[...]
        m_new = jnp.maximum(m_old, jnp.max(s, axis=-1, keepdims=True))
        corr = jnp.exp2(m_old - m_new)
        p = jnp.exp2(s - m_new)
        l_ref[...] = l_ref[...] * corr + jnp.sum(p, axis=-1, keepdims=True)
        acc_ref[...] = acc_ref[...] * corr + _dot(p.astype(BF16), c_blk)
        m_ref[...] = m_new

    half = pages // PAGED_SPLITS
    for g in range(PAGED_SPLITS):
        ks = range(g * half, (g + 1) * half)
        c_blk = jnp.concatenate([c_refs[k][0, 0] for k in ks], axis=0).astype(BF16)
        krt_blk = jnp.concatenate([krt_refs[k][0, 0] for k in ks], axis=1).astype(BF16)
        attend(c_blk, krt_blk, None)

    @pl.when(j == nj - 1)
    def _():
        n = cpad_ref.shape[0]
        cpad_ref[...] = jnp.zeros(cpad_ref.shape, F32)
        krpad_ref[...] = jnp.zeros(krpad_ref.shape, F32)
        cpad_ref[0:steps, :] = cnew_ref[0]
        krpad_ref[0:steps, :] = krnew_ref[0]
        t = lax.broadcasted_iota(jnp.int32, (qrows, n), 0) % steps
        p = lax.broadcasted_iota(jnp.int32, (qrows, n), 1)
        krt_new = krpad_ref[...].T[:ROPE_DIM, :].astype(BF16)
        attend(cpad_ref[...].astype(BF16), krt_new, (p < steps) & (p <= t))
        o_ref[0] = acc_ref[...] / l_ref[...]


def _paged(page_table, qp, qr, c_new, kr_new, wuk, ones8, pool_c, pool_rt, *, layer, pages):
    nreq, qrows, kv_rank = qp.shape
    steps = c_new.shape[1]
    heads = qrows // steps
    n_pages = page_table.shape[1]
    assert qrows % SUBLANES == 0 and qrows <= LANES and n_pages % pages == 0 and pages % PAGED_SPLITS == 0
    flat_pt = page_table.reshape(-1)

    def page_spec(shape, k):
        return pl.BlockSpec((1, 1) + shape, lambda b, j, pt: (layer, pt[b * n_pages + j * pages + k], 0, 0))

    const = lambda shape: pl.BlockSpec(shape, lambda b, j, pt: (0,) * len(shape))
    grid_spec = pltpu.PrefetchScalarGridSpec(
        num_scalar_prefetch=1,
        grid=(nreq, n_pages // pages),
        in_specs=[
            pl.BlockSpec((1, qrows, kv_rank), lambda b, j, pt: (b, 0, 0)),
            pl.BlockSpec((1, qrows, ROPE_DIM), lambda b, j, pt: (b, 0, 0)),
            pl.BlockSpec((1, steps, kv_rank), lambda b, j, pt: (b, 0, 0)),
            pl.BlockSpec((1, steps, LANES), lambda b, j, pt: (b, 0, 0)),
            const(wuk.shape), const(ones8.shape),
        ] + [page_spec((PAGE_SIZE, kv_rank), k) for k in range(pages)]
          + [page_spec((ROPE_DIM, PAGE_SIZE), k) for k in range(pages)],
        out_specs=pl.BlockSpec((1, qrows, kv_rank), lambda b, j, pt: (b, 0, 0)),
        scratch_shapes=[
            pltpu.VMEM((qrows, 1), F32), pltpu.VMEM((qrows, 1), F32), pltpu.VMEM((qrows, kv_rank), F32),
            pltpu.VMEM((LANES, kv_rank), F32), pltpu.VMEM((LANES, LANES), F32),
        ],
    )
    return pl.pallas_call(
        functools.partial(_paged_body, pages=pages, heads=heads, steps=steps),
        grid_spec=grid_spec,
        out_shape=jax.ShapeDtypeStruct((nreq, qrows, kv_rank), F32),
        compiler_params=_cparams(("parallel", "arbitrary"), 48),
        name="paged",
    )(flat_pt, qp, qr, c_new, kr_new, wuk, ones8, *([pool_c] * pages), *([pool_rt] * pages))


def _rope_tables(positions):
    half = ROPE_DIM // 2
    inv = ROPE_BASE ** (-jnp.arange(half, dtype=F32) / half)
    ang = positions[:, None] * inv[None, :]
    cos = jnp.cos(ang).astype(F32)
    sin = jnp.sin(ang).astype(F32)
    reps = LANES // ROPE_DIM
    return jnp.tile(cos, (1, 2 * reps)), jnp.tile(jnp.concatenate([-sin, sin], axis=1), (1, reps))


def _slab(vec, lane0):
    return jnp.zeros((1, LANES), F32).at[0, lane0:lane0 + vec.shape[0]].set(vec.astype(F32))


def kernel(x_prompt, x_sample, cache_mla_latent, cache_mla_rope, state_ssm, state_conv, page_table, p_prompt, p_sample, norm_ffn1, w_ffn1_gate, w_ffn1_up, w_ffn1_down, norm_mix, w_in, gm_norm_v, gm_w_s, gm_b_s, ssm_conv_w, ssm_conv_b, ssm_dt_bias, ssm_a_log, ssm_d, ssm_norm, mla_q_norm, mla_w_uq, mla_qn_norm, mla_qr_norm, mla_kv_norm, mla_kr_norm, mla_w_uk, mla_kn_norm, mla_w_uv, w_out, norm_ffn2, w_ffn2_gate, w_ffn2_up, w_ffn2_down, norm_ple, w_ple_gate, w_ple_proj):
    bp, tp, d_model = x_prompt.shape
    bs, ts, _ = x_sample.shape
    depth = norm_ffn1.shape[0]
    mp, ms = bp * tp, bs * ts
    gm_width = gm_norm_v.shape[1]
    gm_heads = gm_width // GM_HEAD_DIM
    ssm_width = ssm_norm.shape[1]
    ssm_heads = ssm_width // SSM_HEAD_DIM
    conv_ch = ssm_conv_w.shape[2]
    q_rank = mla_q_norm.shape[1]
    kv_rank = mla_kv_norm.shape[1]
    mla_heads = mla_w_uq.shape[2]
    qscale = float((NOPE_DIM + ROPE_DIM) ** -0.5 * np.log2(np.e))

    tm = ms
    col_v = gm_width
    col_z = 2 * gm_width
    col_xbc = col_z + ssm_width
    col_cq = col_xbc + conv_ch
    col_ckv = col_cq + q_rank
    col_slab = col_ckv + kv_rank
    n_in = col_slab + LANES
    dt0 = col_xbc + conv_ch
    assert col_v == gm_width and col_z % ssm_width == 0 and col_xbc % conv_ch == 0
    assert col_cq % q_rank == 0 and col_ckv % kv_rank == 0 and col_slab % LANES == 0
    assert mp % tm == 0 and ms == tm and tp % SSM_CHUNK == 0 and ts < SUBLANES

    pos = jnp.concatenate([jnp.tile(jnp.arange(tp, dtype=F32), bp), jnp.tile(PAST_LEN + jnp.arange(ts, dtype=F32), bs)])
    cos_t, sin_s = _rope_tables(pos)
    bd = jnp.asarray(np.kron(np.eye(LANES // ROPE_DIM), np.ones((ROPE_DIM, ROPE_DIM))), BF16)
    tril = np.tril(np.ones((SSM_CHUNK, SSM_CHUNK)))
    tril3 = jnp.asarray(np.concatenate([tril] * 3, axis=1), BF16)
    e1 = np.zeros((LANES, ssm_width))
    for hh in range(ssm_heads):
        e1[DT_LANE0 + hh, hh * SSM_HEAD_DIM:(hh + 1) * SSM_HEAD_DIM] = 1.0
    e3 = jnp.asarray(np.concatenate([e1] * 3, axis=0), BF16)
    assert LANES % mla_heads == 0 and mla_heads * ts <= LANES
    ones8 = np.zeros((LANES, LANES))
    ones8[:, :mla_heads * ts] = np.kron(np.ones((LANES // mla_heads, 1)), np.kron(np.eye(mla_heads), np.ones((1, ts))))
    ones8 = jnp.asarray(ones8, BF16)
    pool_rt = jnp.swapaxes(cache_mla_rope, 2, 3)

    h = jnp.concatenate([x_prompt.reshape(mp, d_model), x_sample.reshape(ms, d_model)], axis=0)
    outs = [[] for _ in range(8)]
    ssm_all = state_ssm.astype(F32)
    n_prompt_blocks = mp // tm
    step_of = jnp.arange(ms) % ts
    same_req = (jnp.arange(ms)[:, None] // ts) == (jnp.arange(ms)[None, :] // ts)

    for i in range(depth):
        bf = lambda w: w.astype(BF16)
        row = lambda v: v.reshape(1, -1).astype(F32)

        wgu = _cast_gate_up(w_ffn1_gate, w_ffn1_up, layer=i, tf=FFN_TF, tr=CAST_ROWS)
        h = _ffn(h, row(norm_ffn1[i]), wgu, _cast(w_ffn1_down, layer=i, tr=CAST_ROWS // 4), tm=tm)
        w_in_p = _cast_w_in(w_in, layer=i, dt0=dt0, n_dt=ssm_heads, n_out=n_in, tr=CAST_ROWS // 8)
        a = _norm_matmul(h, row(norm_mix[i]), w_in_p, tm=tm, tn=n_in // 5)

        ws = gm_w_s[i]
        wt = bf(jnp.tril(ws))
        w_prompt = jnp.zeros((gm_heads, tm, tm), BF16)
        for cc in range(tm // GM_CHUNK):
            w_prompt = w_prompt.at[:, cc * GM_CHUNK:(cc + 1) * GM_CHUNK, cc * GM_CHUNK:(cc + 1) * GM_CHUNK].set(wt)
        w_samp = jnp.where(same_req, wt[:, step_of[:, None], step_of[None, :]], jnp.zeros((), BF16))
        wbig = jnp.stack([w_prompt, w_samp])
        bias_p = jnp.repeat(jnp.tile(gm_b_s[i].T, (tm // GM_CHUNK, 1)), GM_HEAD_DIM, axis=1)
        bias_s = jnp.repeat(jnp.tile(gm_b_s[i][:, :ts].T, (bs, 1)), GM_HEAD_DIM, axis=1)
        o_gm, vn = _sgu(a, row(gm_norm_v[i]), wbig, jnp.stack([bias_p, bias_s]).astype(F32),
                        rows=tm, n_prompt_blocks=n_prompt_blocks)

        ssd_params = (ssm_conv_w[i].astype(F32), row(ssm_conv_b[i]), _slab(ssm_dt_bias[i], DT_LANE0),
                      _slab(ssm_a_log[i], DT_LANE0), row(jnp.repeat(ssm_d[i], SSM_HEAD_DIM)), row(ssm_norm[i]),
                      tril3, e3)
        o_ssm_p, ssm_p, conv_p = _ssd_prompt(a, ssd_params, batch=bp, seq=tp,
                                             col_z=col_z, col_xbc=col_xbc, col_slab=col_slab)
        a_s = a[mp:]
        cs4 = jnp.pad(state_conv[i].astype(F32), ((0, 0), (0, ts - (SSM_CONV - 1)), (0, 0))).reshape(ms, conv_ch)
        o_ssm_s, ssm_all, conv_s = _ssd_sample(a_s, cs4, ssm_all, ssd_params, layer=i, steps=ts, nreq_blk=16,
                                             col_z=col_z, col_xbc=col_xbc, col_slab=col_slab)

        wuq = mla_w_uq[i]
        wuq_p = jnp.concatenate([wuq[:, :, :NOPE_DIM].reshape(q_rank, -1), wuq[:, :, NOPE_DIM:].reshape(q_rank, -1)], axis=1)
        mla_params = (row(mla_q_norm[i]), bf(wuq_p), row(mla_qn_norm[i]), row(jnp.tile(mla_qr_norm[i], LANES // ROPE_DIM)),
                      row(mla_kv_norm[i]), _slab(mla_kr_norm[i], 0), bd)
        qn, qr, c_lat, k_rope, kr2 = _mla_proj(a, cos_t, sin_s, mla_params, tm=tm,
                                               col_cq=col_cq, col_ckv=col_ckv, col_slab=col_slab, qscale=qscale)
        wuk = mla_w_uk[i]
        wuv = mla_w_uv[i]
        gkn = row(mla_kn_norm[i])
        wuk2 = bf(wuk.reshape(kv_rank, -1))
        k_p, v_p = _kv_proj(c_lat, kr2, wuk2, bf(wuv.reshape(kv_rank, -1)), gkn, rows=mp, tk=tm)
        o_mla_p = _flash(qn, qr, k_p, v_p, batch=bp, seq=tp, blk=tm)

        w_absorb = bf((wuk * mla_kn_norm[i][None, None, :]).transpose(1, 2, 0))
        qp = _absorb(qn, w_absorb, row0=mp, rows=ms)
        qp = qp.reshape(mla_heads, bs, ts, kv_rank).transpose(1, 0, 2, 3).reshape(bs, mla_heads * ts, kv_rank)
        qr_s = qr[mp:].astype(F32).reshape(bs, ts, mla_heads, ROPE_DIM).transpose(0, 2, 1, 3).reshape(bs, mla_heads * ts, ROPE_DIM)
        o_lat = _paged(page_table, qp, qr_s, c_lat[mp:].reshape(bs, ts, kv_rank), k_rope[mp:].reshape(bs, ts, LANES),
                       bf(wuk.transpose(0, 2, 1).reshape(kv_rank, -1)), ones8, cache_mla_latent, pool_rt,
                       layer=i, pages=min(PAGED_PAGES, page_table.shape[1]))
        o_lat = o_lat.reshape(bs, mla_heads, ts, kv_rank).transpose(1, 0, 2, 3).reshape(mla_heads, ms, kv_rank)
        o_mla_s = _v_up(o_lat, bf(wuv.transpose(1, 0, 2)))

        h = _out_proj(h, o_gm, o_ssm_p, o_ssm_s, o_mla_p, o_mla_s, _cast(w_out, layer=i, tr=CAST_ROWS // 4),
                      tm=tm, tn=1024)
        wgu = _cast_gate_up(w_ffn2_gate, w_ffn2_up, layer=i, tf=FFN_TF, tr=CAST_ROWS)
        h = _ffn(h, row(norm_ffn2[i]), wgu, _cast(w_ffn2_down, layer=i, tr=CAST_ROWS // 4), tm=tm)
        pe = jnp.concatenate([p_prompt[i].reshape(mp, -1), p_sample[i].reshape(ms, -1)], axis=0)
        h = _ple(h, row(norm_ple[i]), pe, _cast(w_ple_gate, layer=i, tr=CAST_ROWS // 4), bf(w_ple_proj[i]),
                 tm=tm, tn=1024)

        new = (c_lat[:mp].reshape(bp, tp, kv_rank), k_rope[:mp, :ROPE_DIM].reshape(bp, tp, ROPE_DIM),
               ssm_p, conv_p[:, SUBLANES - (SSM_CONV - 1):],
               c_lat[mp:].reshape(bs, ts, kv_rank), k_rope[mp:, :ROPE_DIM].reshape(bs, ts, ROPE_DIM),
               conv_s.reshape(bs, ts, conv_ch)[:, :SSM_CONV - 1], vn.reshape(bs, ts, gm_width))
        for lst, val in zip(outs, new):
            lst.append(val)

    stacked = [jnp.stack(o) for o in outs]
    stacked.insert(6, ssm_all.astype(state_ssm.dtype))
    return (h[:mp].reshape(bp, tp, d_model), h[mp:].reshape(bs, ts, d_model)) + tuple(stacked)
```

```python
import functools

import numpy as np
import jax
import jax.numpy as jnp
from jax import lax
from jax.experimental import pallas as pl
from jax.experimental.pallas import tpu as pltpu

F32 = jnp.float32
BF16 = jnp.bfloat16

EPS = 1e-6
ROPE_BASE = 10000.0
PAST_LEN = 8192
PAGE_SIZE = 128

LANES = 128
SUBLANES = 8
MIB = 1024 * 1024

GM_HEAD_DIM = 128
GM_CHUNK = 128
SSM_HEAD_DIM = 64
SSM_GROUPS = 2
SSM_STATE = 128
SSM_CONV = 4
SSM_CHUNK = 128
NOPE_DIM = 128
ROPE_DIM = 64
V_DIM = 128
Q_BLOCK = 128

DT_LANE0 = ROPE_DIM


def _cparams(sem, vmem_mib):
    return pltpu.CompilerParams(dimension_semantics=sem, vmem_limit_bytes=int(vmem_mib * MIB))


def _nt(a, b):
    return lax.dot_general(a, b, (((1,), (1,)), ((), ())), preferred_element_type=F32)


def _dot(a, b):
    return jnp.dot(a, b, preferred_element_type=F32)


def _silu(x):
    return x * jax.nn.sigmoid(x)


def _softplus(x):
    return jnp.maximum(x, 0.0) + jnp.log1p(jnp.exp(-jnp.abs(x)))


def _gelu_tanh(x):
    return 0.5 * x * (1.0 + jnp.tanh(np.sqrt(2.0 / np.pi).astype(np.float32) * (x + 0.044715 * (x * x * x))))


def _rms(x, gain):
    return x * lax.rsqrt(jnp.mean(x * x, axis=-1, keepdims=True) + EPS) * gain


def _split3(x):
    hi = x.astype(BF16)
    r1 = x - hi.astype(F32)
    mid = r1.astype(BF16)
    lo = (r1 - mid.astype(F32)).astype(BF16)
    return hi, mid, lo


def _expand_heads(x, e3):
    return _dot(jnp.concatenate(_split3(x), axis=1), e3)


FFN_TF = 512
CAST_ROWS = 2048
FFN_DOWN_CHUNK = 1024
NORM_ROWS = 64


def _norm_rows(x_ref, g_ref, xn_ref, copy_ref=None):
    g = g_ref[...]
    for r in range(x_ref.shape[0] // NORM_ROWS):
        rows = slice(r * NORM_ROWS, (r + 1) * NORM_ROWS)
        x = x_ref[rows, :]
        xn_ref[rows, :] = _rms(x, g).astype(BF16)
        if copy_ref is not None:
            copy_ref[rows, :] = x


def _ffn_body(x_ref, g_ref, wgu_ref, wd_ref, o_ref, xn_ref):
    @pl.when(pl.program_id(1) == 0)
    def _():
        _norm_rows(x_ref, g_ref, xn_ref, o_ref)

    tf = wd_ref.shape[0]
    gu = _dot(xn_ref[...], wgu_ref[0])
    act = (_silu(gu[:, :tf]) * gu[:, tf:] * 0.5).astype(BF16)
    dc = FFN_DOWN_CHUNK
    for n in range(o_ref.shape[1] // dc):
        o_ref[:, n * dc:(n + 1) * dc] += _dot(act, wd_ref[:, n * dc:(n + 1) * dc])


def _ffn(h, g, wgu, wd, *, tm):
    m, d = h.shape
    nf, _, tf2 = wgu.shape
    tf = tf2 // 2
    vmem = (tm * d * 4 * 3 + tm * d * 2 + 3 * 2 * d * tf * 2 + 3 * tm * tf * 4 + 2 * tm * FFN_DOWN_CHUNK * 4) / MIB + 3
    return pl.pallas_call(
        _ffn_body,
        grid=(m // tm, nf),
        in_specs=[
            pl.BlockSpec((tm, d), lambda i, j: (i, 0), pipeline_mode=pl.Buffered(1)),
            pl.BlockSpec((1, d), lambda i, j: (0, 0)),
            pl.BlockSpec((1, d, tf2), lambda i, j: (j, 0, 0)),
            pl.BlockSpec((tf, d), lambda i, j: (j, 0)),
        ],
        out_specs=pl.BlockSpec((tm, d), lambda i, j: (i, 0)),
        out_shape=jax.ShapeDtypeStruct((m, d), F32),
        scratch_shapes=[pltpu.VMEM((tm, d), BF16)],
        compiler_params=_cparams(("parallel", "arbitrary"), vmem),
        name="ffn",
    )(h, g, wgu, wd)


def _cast_gate_up_body(wg_ref, wu_ref, o_ref):
    tf = wg_ref.shape[2]
    o_ref[0, :, :tf] = wg_ref[0].astype(BF16)
    o_ref[0, :, tf:] = wu_ref[0].astype(BF16)


def _cast_gate_up(wg_all, wu_all, *, layer, tf, tr):
    _, d, f = wg_all.shape
    spec = pl.BlockSpec((1, tr, tf), lambda j, r: (layer, r, j))
    return pl.pallas_call(
        _cast_gate_up_body,
        grid=(f // tf, d // tr),
        in_specs=[spec, spec],
        out_specs=pl.BlockSpec((1, tr, 2 * tf), lambda j, r: (j, r, 0)),
        out_shape=jax.ShapeDtypeStruct((f // tf, d, 2 * tf), BF16),
        compiler_params=_cparams(("parallel", "parallel"), (2 * 2 * tr * tf * 4 + 2 * tr * 2 * tf * 2) / MIB + 4),
        name="cast_gate_up",
    )(wg_all, wu_all)


def _cast_body(w_ref, o_ref):
    o_ref[...] = w_ref[0].astype(BF16)


def _cast(w_all, *, layer, tr):
    _, r, c = w_all.shape
    return pl.pallas_call(
        _cast_body,
        grid=(r // tr,),
        in_specs=[pl.BlockSpec((1, tr, c), lambda i: (layer, i, 0))],
        out_specs=pl.BlockSpec((tr, c), lambda i: (i, 0)),
        out_shape=jax.ShapeDtypeStruct((r, c), BF16),
        compiler_params=_cparams(("parallel",), (2 * tr * c * 4 + 2 * tr * c * 2) / MIB + 4),
        name="cast",
    )(w_all)


def _cast_w_in_body(w_ref, o_ref, *, dt0, n_dt):
    x = w_ref[0]
    n = x.shape[1]
    tail = n - dt0 - n_dt
    o_ref[:, :dt0] = x[:, :dt0].astype(BF16)
    o_ref[:, dt0:dt0 + tail] = x[:, dt0 + n_dt:].astype(BF16)
    o_ref[:, dt0 + tail:n] = x[:, dt0:dt0 + n_dt].astype(BF16)
    o_ref[:, n:] = jnp.zeros((x.shape[0], o_ref.shape[1] - n), BF16)


def _cast_w_in(w_all, *, layer, dt0, n_dt, n_out, tr):
    _, r, c = w_all.shape
    return pl.pallas_call(
        functools.partial(_cast_w_in_body, dt0=dt0, n_dt=n_dt),
        grid=(r // tr,),
        in_specs=[pl.BlockSpec((1, tr, c), lambda i: (layer, i, 0))],
        out_specs=pl.BlockSpec((tr, n_out), lambda i: (i, 0)),
        out_shape=jax.ShapeDtypeStruct((r, n_out), BF16),
        compiler_params=_cparams(("parallel",), (3 * tr * c * 4 + 3 * tr * n_out * 2) / MIB + 4),
        name="cast_w_in",
    )(w_all)


def _norm_mm_body(x_ref, g_ref, w_ref, o_ref, xn_ref):
    @pl.when(pl.program_id(1) == 0)
    def _():
        _norm_rows(x_ref, g_ref, xn_ref)

    o_ref[...] = _dot(xn_ref[...], w_ref[...])


def _norm_matmul(h, g, w, *, tm, tn):
    m, d = h.shape
    n = w.shape[1]
    vmem = (2 * tm * d * 4 + tm * d * 2 + 2 * d * tn * 2 + 3 * tm * tn * 4) / MIB + 4
    return pl.pallas_call(
        _norm_mm_body,
        grid=(m // tm, n // tn),
        in_specs=[
            pl.BlockSpec((tm, d), lambda i, j: (i, 0), pipeline_mode=pl.Buffered(1)),
            pl.BlockSpec((1, d), lambda i, j: (0, 0)),
            pl.BlockSpec((d, tn), lambda i, j: (0, j)),
        ],
        out_specs=pl.BlockSpec((tm, tn), lambda i, j: (i, j)),
        out_shape=jax.ShapeDtypeStruct((m, n), F32),
        scratch_shapes=[pltpu.VMEM((tm, d), BF16)],
        compiler_params=_cparams(("parallel", "arbitrary"), vmem),
        name="in_proj",
    )(h, g, w)


def _out_proj_body(gm_ref, ssm_p_ref, ssm_s_ref, mla_p_ref, mla_s_ref, w_ref, res_ref, o_ref, *, n_prompt_blocks):
    is_sample = pl.program_id(0) >= n_prompt_blocks
    ssm = jnp.where(is_sample, ssm_s_ref[...], ssm_p_ref[...])
    mla = jnp.where(is_sample, mla_s_ref[...], mla_p_ref[...])
    k1 = gm_ref.shape[1]
    k2 = k1 + ssm.shape[1]
    o_ref[...] = (res_ref[...] + _dot(gm_ref[...], w_ref[0:k1, :]) + _dot(ssm, w_ref[k1:k2, :])
                  + _dot(mla, w_ref[k2:, :]))


def _out_proj(h, gm, ssm_p, ssm_s, mla_p, mla_s, w, *, tm, tn):
    m, d = h.shape
    k = w.shape[0]
    npb = ssm_p.shape[0] // tm
    assert ssm_s.shape[0] == tm and mla_s.shape[0] == tm
    vmem = (2 * 2 * tm * k * 2 + 2 * k * tn * 2 + 6 * tm * tn * 4) / MIB + 4
    prompt = lambda width: pl.BlockSpec((tm, width), lambda i, j: (jnp.minimum(i, npb - 1), 0))
    sample = lambda width: pl.BlockSpec((tm, width), lambda i, j: (0, 0))
    return pl.pallas_call(
        functools.partial(_out_proj_body, n_prompt_blocks=npb),
        grid=(m // tm, d // tn),
        in_specs=[
            pl.BlockSpec((tm, gm.shape[1]), lambda i, j: (i, 0)),
            prompt(ssm_p.shape[1]), sample(ssm_s.shape[1]), prompt(mla_p.shape[1]), sample(mla_s.shape[1]),
            pl.BlockSpec((k, tn), lambda i, j: (0, j)),
            pl.BlockSpec((tm, tn), lambda i, j: (i, j)),
        ],
        out_specs=pl.BlockSpec((tm, tn), lambda i, j: (i, j)),
        out_shape=jax.ShapeDtypeStruct((m, d), F32),
        compiler_params=_cparams(("parallel", "arbitrary"), vmem),
        name="out_proj",
    )(gm, ssm_p, ssm_s, mla_p, mla_s, w, h)


def _ple_body(x_ref, g_ref, pe_ref, wg_ref, wp_ref, res_ref, o_ref, xn_ref):
    @pl.when(pl.program_id(1) == 0)
    def _():
        _norm_rows(x_ref, g_ref, xn_ref)

    gate = jax.nn.sigmoid(_dot(xn_ref[...], wg_ref[...]))
    proj = _dot(pe_ref[...].astype(BF16), wp_ref[...])
    o_ref[...] = res_ref[...] + gate * proj


def _ple(h, g, pe, wg, wp, *, tm, tn):
    m, d = h.shape
    dp = pe.shape[1]
    vmem = (2 * tm * d * 4 + tm * d * 2 + 2 * tm * dp * 4 + 2 * d * tn * 2 + 2 * dp * tn * 2 + 8 * tm * tn * 4) / MIB + 4
    return pl.pallas_call(
        _ple_body,
        grid=(m // tm, d // tn),
        in_specs=[
            pl.BlockSpec((tm, d), lambda i, j: (i, 0), pipeline_mode=pl.Buffered(1)),
            pl.BlockSpec((1, d), lambda i, j: (0, 0)),
            pl.BlockSpec((tm, dp), lambda i, j: (i, 0)),
            pl.BlockSpec((d, tn), lambda i, j: (0, j)),
            pl.BlockSpec((dp, tn), lambda i, j: (0, j)),
            pl.BlockSpec((tm, tn), lambda i, j: (i, j)),
        ],
        out_specs=pl.BlockSpec((tm, tn), lambda i, j: (i, j)),
        out_shape=jax.ShapeDtypeStruct((m, d), F32),
        scratch_shapes=[pltpu.VMEM((tm, d), BF16)],
        compiler_params=_cparams(("parallel", "arbitrary"), vmem),
        name="ple",
    )(h, g, pe, wg, wp, h)


def _sgu_body(u_ref, v_ref, gain_ref, w_ref, bias_ref, o_ref, vn_ref, *, heads):
    u = _gelu_tanh(u_ref[...])
    v = _gelu_tanh(v_ref[...])
    for g in range(heads):
        sl = slice(g * GM_HEAD_DIM, (g + 1) * GM_HEAD_DIM)
        vn = _rms(v[:, sl], gain_ref[:, sl])
        vn_ref[:, sl] = vn
        mixed = _dot(w_ref[0, g], vn.astype(BF16)) + bias_ref[0, :, sl]
        o_ref[:, sl] = (u[:, sl] * mixed).astype(o_ref.dtype)


def _sgu(a, gain, wbig, bias, *, rows, n_prompt_blocks):
    m = a.shape[0]
    width = gain.shape[1]
    heads = width // GM_HEAD_DIM
    sel = lambda i: jnp.minimum(i // n_prompt_blocks, 1)
    return pl.pallas_call(
        functools.partial(_sgu_body, heads=heads),
        grid=(m // rows,),
        in_specs=[
            pl.BlockSpec((rows, width), lambda i: (i, 0)),
            pl.BlockSpec((rows, width), lambda i: (i, 1)),
            pl.BlockSpec((1, width), lambda i: (0, 0)),
            pl.BlockSpec((1, heads, rows, rows), lambda i: (sel(i), 0, 0, 0)),
            pl.BlockSpec((1, rows, width), lambda i: (sel(i), 0, 0)),
        ],
        out_specs=[
            pl.BlockSpec((rows, width), lambda i: (i, 0)),
            pl.BlockSpec((rows, width), lambda i: (0, 0)),
        ],
        out_shape=[jax.ShapeDtypeStruct((m, width), BF16), jax.ShapeDtypeStruct((rows, width), F32)],
        compiler_params=_cparams(("arbitrary",), 40),
        name="sgu",
    )(a, a, gain, wbig, bias)


def _ssd_common_tail(y, xs, z, dfull, norm, o_ref, width):
    y = y + xs * dfull
    gt = y * _silu(z)
    gw = width // SSM_GROUPS
    for g in range(SSM_GROUPS):
        sl = slice(g * gw, (g + 1) * gw)
        seg = gt[:, sl]
        o_ref[:, sl] = (seg * lax.rsqrt(jnp.mean(seg * seg, axis=-1, keepdims=True) + EPS) * norm[:, sl]).astype(o_ref.dtype)


def _dt_slab(slab_ref, dtb_ref, alog_ref, heads):
    lane = lax.broadcasted_iota(jnp.int32, slab_ref.shape, 1)
    raw = jnp.where((lane >= DT_LANE0) & (lane < DT_LANE0 + heads), slab_ref[...], 0.0)
    dt = _softplus(raw + dtb_ref[...])
    return dt, dt * (-jnp.exp(alog_ref[...]))


def _ssd_prompt_body(z_ref, xbc_ref, slab_ref, cw_ref, cb_ref, dtb_ref, alog_ref, dfull_ref, norm_ref, tril3_ref,
                     e3_ref, o_ref, st_out_ref, conv_out_ref, full_ref, state_ref, y_ref, *, chunk, n_chunks, heads):
    c = pl.program_id(1)
    width = heads * SSM_HEAD_DIM
    hpg = heads // SSM_GROUPS
    gw = hpg * SSM_HEAD_DIM
    tail = SUBLANES

    @pl.when(c == 0)
    def _():
        full_ref[0:tail, :] = jnp.zeros((tail, full_ref.shape[1]), F32)
        state_ref[...] = jnp.zeros(state_ref.shape, F32)

    full_ref[tail:tail + chunk, :] = xbc_ref[...]
    conv = cb_ref[...]
    for k in range(SSM_CONV):
        conv = conv + cw_ref[k:k + 1, :] * full_ref[pl.ds(tail - (SSM_CONV - 1) + k, chunk), :]
    full_ref[0:tail, :] = full_ref[chunk:chunk + tail, :]
    xc = _silu(conv)
    xs = xc[:, :width]
    bm = xc[:, width:width + SSM_GROUPS * SSM_STATE].astype(BF16)
    cm = xc[:, width + SSM_GROUPS * SSM_STATE:].astype(BF16)

    dt, a = _dt_slab(slab_ref, dtb_ref, alog_ref, heads)
    acs = _dot(tril3_ref[...], jnp.concatenate(_split3(a), axis=0))
    acs_t = acs.T
    arem = acs[chunk - 1:chunk, :] - acs
    e3 = e3_ref[...]
    xdt = xs * _expand_heads(dt, e3)
    ecs = jnp.exp(_expand_heads(acs, e3))
    xw_t = (xdt * jnp.exp(_expand_heads(arem, e3))).T.astype(BF16)
    ecol = ecs.T[:, chunk - 1:chunk]
    xdt_b = xdt.astype(BF16)

    row_i = lax.broadcasted_iota(jnp.int32, (chunk, chunk), 0)
    col_i = lax.broadcasted_iota(jnp.int32, (chunk, chunk), 1)
    causal = row_i >= col_i
    for g in range(SSM_GROUPS):
        cg = cm[:, g * SSM_STATE:(g + 1) * SSM_STATE]
        bg = bm[:, g * SSM_STATE:(g + 1) * SSM_STATE]
        cb = _nt(cg, bg)
        st = state_ref[g * hpg:(g + 1) * hpg].reshape(gw, SSM_STATE)
        y_ref[:, g * gw:(g + 1) * gw] = _nt(cg, st.astype(BF16)) * ecs[:, g * gw:(g + 1) * gw]
        for hh in range(hpg):
            h = g * hpg + hh
            col = acs[:, DT_LANE0 + h:DT_LANE0 + h + 1]
            row = acs_t[DT_LANE0 + h:DT_LANE0 + h + 1, :]
            mat = jnp.where(causal, cb * jnp.exp(col - row), 0.0).astype(BF16)
            hs = slice(h * SSM_HEAD_DIM, (h + 1) * SSM_HEAD_DIM)
            y_ref[:, hs] += _dot(mat, xdt_b[:, hs])
        s_new = _dot(xw_t[g * gw:(g + 1) * gw, :], bg)
        state_ref[g * hpg:(g + 1) * hpg] = (st * ecol[g * gw:(g + 1) * gw, :] + s_new).reshape(hpg, SSM_HEAD_DIM, SSM_STATE)

    _ssd_common_tail(y_ref[...], xs, z_ref[...], dfull_ref[...], norm_ref[...], o_ref, width)

    @pl.when(c == n_chunks - 1)
    def _():
        st_out_ref[0] = state_ref[...]
        conv_out_ref[0] = full_ref[0:tail, :]


def _ssd_prompt(a, params, *, batch, seq, col_z, col_xbc, col_slab):
    cw, cb, dtb, alog, dfull, norm, tril3, e3 = params
    width = norm.shape[1]
    heads = width // SSM_HEAD_DIM
    conv_ch = cw.shape[1]
    chunk = SSM_CHUNK
    nc = seq // chunk
    const = lambda shape: pl.BlockSpec(shape, lambda b, c: (0,) * len(shape))
    rowblk = lambda b, c: b * nc + c
    return pl.pallas_call(
        functools.partial(_ssd_prompt_body, chunk=chunk, n_chunks=nc, heads=heads),
        grid=(batch, nc),
        in_specs=[
            pl.BlockSpec((chunk, width), lambda b, c: (rowblk(b, c), col_z // width)),
            pl.BlockSpec((chunk, conv_ch), lambda b, c: (rowblk(b, c), col_xbc // conv_ch)),
            pl.BlockSpec((chunk, LANES), lambda b, c: (rowblk(b, c), col_slab // LANES)),
            const(cw.shape), const(cb.shape), const(dtb.shape), const(alog.shape), const(dfull.shape),
            const(norm.shape), const(tril3.shape), const(e3.shape),
        ],
        out_specs=[
            pl.BlockSpec((chunk, width), lambda b, c: (rowblk(b, c), 0)),
            pl.BlockSpec((1, heads, SSM_HEAD_DIM, SSM_STATE), lambda b, c: (b, 0, 0, 0)),
            pl.BlockSpec((1, SUBLANES, conv_ch), lambda b, c: (b, 0, 0)),
        ],
        out_shape=[
            jax.ShapeDtypeStruct((batch * seq, width), BF16),
            jax.ShapeDtypeStruct((batch, heads, SSM_HEAD_DIM, SSM_STATE), F32),
            jax.ShapeDtypeStruct((batch, SUBLANES, conv_ch), F32),
        ],
        scratch_shapes=[
            pltpu.VMEM((chunk + SUBLANES, conv_ch), F32),
            pltpu.VMEM((heads, SSM_HEAD_DIM, SSM_STATE), F32),
            pltpu.VMEM((chunk, width), F32),
        ],
        compiler_params=_cparams(("parallel", "arbitrary"), 40),
        name="ssd_prompt",
    )(a, a, a, cw, cb, dtb, alog, dfull, norm, tril3, e3)


def _ssd_sample_body(z_ref, xbc_ref, slab_ref, cs_ref, st_ref, cw_ref, cb_ref, dtb_ref, alog_ref, dfull_ref, norm_ref,
                     e3_ref, o_ref, st_out_ref, conv_out_ref, *, rows, steps, heads):
    width = heads * SSM_HEAD_DIM
    hpg = heads // SSM_GROUPS
    gw = hpg * SSM_HEAD_DIM
    nreq = rows // steps
    x = xbc_ref[...]
    cs = cs_ref[...]
    t_of = lambda shape: lax.broadcasted_iota(jnp.int32, shape, 0) % steps

    def back(v, j):
        return v if j == 0 else pltpu.roll(v, j, axis=0)

    def fwd(v, j):
        return v if j == 0 else pltpu.roll(v, rows - j, axis=0)

    tx = t_of(x.shape)
    conv = cb_ref[...] + cw_ref[SSM_CONV - 1:SSM_CONV, :] * x
    for k in range(SSM_CONV - 1):
        j = SSM_CONV - 1 - k
        conv = conv + cw_ref[k:k + 1, :] * jnp.where(tx >= j, back(x, j), fwd(cs, k))
    conv_out_ref[...] = fwd(x, steps - (SSM_CONV - 1))
    xc = _silu(conv)
    xs = xc[:, :width]
    bm = xc[:, width:width + SSM_GROUPS * SSM_STATE]
    cm = xc[:, width + SSM_GROUPS * SSM_STATE:]

    dt, a = _dt_slab(slab_ref, dtb_ref, alog_ref, heads)
    ts = t_of(a.shape)
    acs = a
    arem = jnp.zeros_like(a)
    for j in range(1, steps):
        acs = acs + jnp.where(ts >= j, back(a, j), 0.0)
        arem = arem + jnp.where(ts < steps - j, fwd(a, j), 0.0)
    e3 = e3_ref[...]
    acsx = _expand_heads(acs, e3)
    xdt = xs * _expand_heads(dt, e3)
    ecs = jnp.exp(acsx)
    xw = xdt * jnp.exp(_expand_heads(arem, e3))

    tw = t_of(xs.shape)
    y = jnp.zeros_like(xs)
    for j in range(steps):
        bj = back(bm, j)
        dec = jnp.exp(acsx - back(acsx, j)) * back(xdt, j)
        parts = []
        for g in range(SSM_GROUPS):
            sl = slice(g * SSM_STATE, (g + 1) * SSM_STATE)
            cbj = jnp.sum(cm[:, sl] * bj[:, sl], axis=-1, keepdims=True)
            parts.append(cbj * dec[:, g * gw:(g + 1) * gw])
        y = y + jnp.where(tw >= j, jnp.concatenate(parts, axis=1), 0.0)

    pad = jnp.zeros((LANES - rows, width), F32)
    xw_t = jnp.concatenate([xw, pad], axis=0).T.astype(BF16)
    e_t = jnp.concatenate([ecs, pad], axis=0).T
    cm_b = cm.astype(BF16)
    bm_pad = jnp.concatenate([bm, jnp.zeros((LANES - rows, bm.shape[1]), F32)], axis=0)
    req_y = lax.broadcasted_iota(jnp.int32, (rows, gw), 0) // steps
    req_b = lax.broadcasted_iota(jnp.int32, (LANES, SSM_STATE), 0) // steps
    yoff = [jnp.zeros((rows, gw), F32) for _ in range(SSM_GROUPS)]
    for b in range(nreq):
        last = b * steps + steps - 1
        for g in range(SSM_GROUPS):
            sl = slice(g * SSM_STATE, (g + 1) * SSM_STATE)
            h0 = st_ref[0, b, g * hpg:(g + 1) * hpg].reshape(gw, SSM_STATE)
            yoff[g] = yoff[g] + jnp.where(req_y == b, _nt(cm_b[:, sl], h0.astype(BF16)), 0.0)
            b_only = jnp.where(req_b == b, bm_pad[:, sl], 0.0).astype(BF16)
            s_new = _dot(xw_t[g * gw:(g + 1) * gw, :], b_only)
            decay = e_t[g * gw:(g + 1) * gw, last:last + 1]
            st_out_ref[0, b, g * hpg:(g + 1) * hpg] = (h0 * decay + s_new).reshape(hpg, SSM_HEAD_DIM, SSM_STATE)
    y = y + jnp.concatenate(yoff, axis=1) * ecs

    _ssd_common_tail(y, xs, z_ref[...], dfull_ref[...], norm_ref[...], o_ref, width)


def _ssd_sample(a_s, cs4, state, params, *, layer, steps, nreq_blk, col_z, col_xbc, col_slab):
    cw, cb, dtb, alog, dfull, norm, _, e3 = params
    width = norm.shape[1]
    heads = width // SSM_HEAD_DIM
    conv_ch = cw.shape[1]
    m = a_s.shape[0]
    rows = nreq_blk * steps
    const = lambda shape: pl.BlockSpec(shape, lambda i: (0,) * len(shape))
    st_spec = pl.BlockSpec((1, nreq_blk, heads, SSM_HEAD_DIM, SSM_STATE), lambda i: (layer, i, 0, 0, 0))
    return pl.pallas_call(
        functools.partial(_ssd_sample_body, rows=rows, steps=steps, heads=heads),
        grid=(m // rows,),
        in_specs=[
            pl.BlockSpec((rows, width), lambda i: (i, col_z // width)),
            pl.BlockSpec((rows, conv_ch), lambda i: (i, col_xbc // conv_ch)),
            pl.BlockSpec((rows, LANES), lambda i: (i, col_slab // LANES)),
            pl.BlockSpec((rows, conv_ch), lambda i: (i, 0)),
            st_spec,
            const(cw.shape), const(cb.shape), const(dtb.shape), const(alog.shape), const(dfull.shape),
            const(norm.shape), const(e3.shape),
        ],
        out_specs=[
            pl.BlockSpec((rows, width), lambda i: (i, 0)),
            st_spec,
            pl.BlockSpec((rows, conv_ch), lambda i: (i, 0)),
        ],
        out_shape=[
            jax.ShapeDtypeStruct((m, width), BF16),
            jax.ShapeDtypeStruct(state.shape, F32),
            jax.ShapeDtypeStruct((m, conv_ch), F32),
        ],
        input_output_aliases={4: 1},
        compiler_params=_cparams(("parallel",), 48),
        name="ssd_sample",
    )(a_s, a_s, a_s, cs4, state, cw, cb, dtb, alog, dfull, norm, e3)


def _rms_pairs(x, bd):
    s = x * x
    hi = s.astype(BF16)
    lo = (s - hi.astype(F32)).astype(BF16)
    ssq = _dot(hi, bd) + _dot(lo, bd)
    return x * lax.rsqrt(ssq * (1.0 / ROPE_DIM) + EPS)


def _rope_pairs(y, cos_t, sin_s):
    lane = lax.broadcasted_iota(jnp.int32, y.shape, 1)
    half = ROPE_DIM // 2
    rot = jnp.where((lane % ROPE_DIM) < half, pltpu.roll(y, LANES - half, axis=1), pltpu.roll(y, half, axis=1))
    return y * cos_t + rot * sin_s


def _mla_proj_body(cq_ref, ckv_ref, slab_ref, cos_ref, sin_ref, gq_ref, wuq_ref, gqn_ref, gqr_ref, gkv_ref, gkr_ref,
                   bd_ref, qn_ref, qr_ref, c_ref, kr_ref, kr2_ref, *, heads, qscale):
    cqn = _rms(cq_ref[...], gq_ref[...]).astype(BF16)
    q = _dot(cqn, wuq_ref[...])
    gqn = gqn_ref[...]
    for h in range(heads):
        sl = slice(h * NOPE_DIM, (h + 1) * NOPE_DIM)
        qn_ref[:, sl] = (_rms(q[:, sl], gqn) * qscale).astype(BF16)
    bd = bd_ref[...]
    cos_t = cos_ref[...]
    sin_s = sin_ref[...]
    base = heads * NOPE_DIM
    for j in range(heads * ROPE_DIM // LANES):
        x = q[:, base + j * LANES: base + (j + 1) * LANES]
        y = _rope_pairs(_rms_pairs(x, bd) * gqr_ref[...], cos_t, sin_s)
        qr_ref[:, j * LANES:(j + 1) * LANES] = (y * qscale).astype(BF16)
    c_ref[...] = _rms(ckv_ref[...], gkv_ref[...])
    kr = _rope_pairs(_rms_pairs(slab_ref[...], bd) * gkr_ref[...], cos_t, sin_s)
    kr_ref[...] = kr
    kr2_ref[...] = (kr + pltpu.roll(kr, ROPE_DIM, axis=1)).astype(BF16)


def _mla_proj(a, cos_t, sin_s, params, *, tm, col_cq, col_ckv, col_slab, qscale):
    gq, wuq, gqn, gqr, gkv, gkr, bd = params
    m = a.shape[0]
    q_rank = gq.shape[1]
    kv_rank = gkv.shape[1]
    heads = wuq.shape[1] // (NOPE_DIM + ROPE_DIM)
    const = lambda shape: pl.BlockSpec(shape, lambda i: (0,) * len(shape))
    row = lambda w: pl.BlockSpec((tm, w), lambda i: (i, 0))
    return pl.pallas_call(
        functools.partial(_mla_proj_body, heads=heads, qscale=qscale),
        grid=(m // tm,),
        in_specs=[
            pl.BlockSpec((tm, q_rank), lambda i: (i, col_cq // q_rank)),
            pl.BlockSpec((tm, kv_rank), lambda i: (i, col_ckv // kv_rank)),
            pl.BlockSpec((tm, LANES), lambda i: (i, col_slab // LANES)),
            row(LANES), row(LANES),
            const(gq.shape), const(wuq.shape), const(gqn.shape), const(gqr.shape), const(gkv.shape),
            const(gkr.shape), const(bd.shape),
        ],
        out_specs=[row(heads * NOPE_DIM), row(heads * ROPE_DIM), row(kv_rank), row(LANES), row(LANES)],
        out_shape=[
            jax.ShapeDtypeStruct((m, heads * NOPE_DIM), BF16),
            jax.ShapeDtypeStruct((m, heads * ROPE_DIM), BF16),
            jax.ShapeDtypeStruct((m, kv_rank), F32),
            jax.ShapeDtypeStruct((m, LANES), F32),
            jax.ShapeDtypeStruct((m, LANES), BF16),
        ],
        compiler_params=_cparams(("parallel",), 48),
        name="mla_proj",
    )(a, a, a, cos_t, sin_s, gq, wuq, gqn, gqr, gkv, gkr, bd)


def _kv_proj_body(c_ref, kr2_ref, wuk_ref, wuv_ref, gkn_ref, k_ref, v_ref, *, heads):
    cb = c_ref[...].astype(BF16)
    k = _dot(cb, wuk_ref[...])
    gkn = gkn_ref[...]
    kr2 = kr2_ref[...]
    for h in range(heads):
        sl = slice(h * NOPE_DIM, (h + 1) * NOPE_DIM)
        k_ref[:, 2 * h * NOPE_DIM:(2 * h + 1) * NOPE_DIM] = _rms(k[:, sl], gkn).astype(BF16)
        k_ref[:, (2 * h + 1) * NOPE_DIM:(2 * h + 2) * NOPE_DIM] = kr2
    v_ref[...] = _dot(cb, wuv_ref[...]).astype(BF16)


def _kv_proj(c, kr2, wuk, wuv, gkn, *, rows, tk):
    kv_rank = c.shape[1]
    n = wuk.shape[1]
    heads = n // NOPE_DIM
    const = lambda shape: pl.BlockSpec(shape, lambda i: (0,) * len(shape))
    return pl.pallas_call(
        functools.partial(_kv_proj_body, heads=heads),
        grid=(rows // tk,),
        in_specs=[pl.BlockSpec((tk, kv_rank), lambda i: (i, 0)), pl.BlockSpec((tk, LANES), lambda i: (i, 0)),
                  const(wuk.shape), const(wuv.shape), const(gkn.shape)],
        out_specs=[pl.BlockSpec((tk, 2 * n), lambda i: (i, 0)), pl.BlockSpec((tk, n), lambda i: (i, 0))],
        out_shape=[jax.ShapeDtypeStruct((rows, 2 * n), BF16), jax.ShapeDtypeStruct((rows, n), BF16)],
        compiler_params=_cparams(("parallel",), 32),
        name="kv_proj",
    )(c, kr2, wuk, wuv, gkn)


FLASH_HEADS = 2


def _flash_body(qn_ref, qr_ref, k_ref, v_ref, o_ref, m_ref, l_ref, acc_ref, *, blk):
    i = pl.program_id(2)
    lane = lax.broadcasted_iota(jnp.int32, qr_ref.shape, 1)
    zero = jnp.zeros_like(qr_ref[...])
    q = []
    for hh in range(FLASH_HEADS):
        qr = jnp.where((lane // ROPE_DIM) == hh, qr_ref[...], zero)
        q.append(jnp.concatenate([qn_ref[:, hh * NOPE_DIM:(hh + 1) * NOPE_DIM], qr], axis=1))
    m_ref[...] = jnp.full(m_ref.shape, -jnp.inf, F32)
    l_ref[...] = jnp.zeros(l_ref.shape, F32)
    acc_ref[...] = jnp.zeros(acc_ref.shape, F32)
    kw = NOPE_DIM + LANES
    reps = blk // LANES

    def step(j, masked):
        rows = pl.ds(pl.multiple_of(j * blk, blk), blk)
        for hh in range(FLASH_HEADS):
            s = _nt(q[hh], k_ref[rows, hh * kw:(hh + 1) * kw])
            if masked:
                r = lax.broadcasted_iota(jnp.int32, s.shape, 0)
                c = lax.broadcasted_iota(jnp.int32, s.shape, 1)
                s = jnp.where(c <= r, s, -jnp.inf)
            m_old = m_ref[hh]
            m_new = jnp.maximum(m_old, jnp.max(s, axis=-1, keepdims=True))
            corr = jnp.exp2(m_old - m_new)
            p = jnp.exp2(s - jnp.tile(m_new, (1, reps)))
            l_ref[hh] = l_ref[hh] * corr + jnp.sum(p, axis=-1, keepdims=True)
            acc_ref[hh] = acc_ref[hh] * corr + _dot(p.astype(BF16), v_ref[rows, hh * V_DIM:(hh + 1) * V_DIM])
            m_ref[hh] = m_new

    def body(j, carry):
        step(j, False)
        return carry

    lax.fori_loop(0, i, body, 0)
    step(i, True)
    for hh in range(FLASH_HEADS):
        o_ref[:, hh * V_DIM:(hh + 1) * V_DIM] = (acc_ref[hh] / l_ref[hh]).astype(o_ref.dtype)


def _flash(qn, qr, kcat, v, *, batch, seq, blk):
    heads = v.shape[1] // V_DIM
    nb = seq // blk
    hp = FLASH_HEADS
    return pl.pallas_call(
        functools.partial(_flash_body, blk=blk),
        grid=(batch, heads // hp, nb),
        in_specs=[
            pl.BlockSpec((blk, hp * NOPE_DIM), lambda b, h, i: (b * nb + i, h)),
            pl.BlockSpec((blk, LANES), lambda b, h, i: (b * nb + i, h)),
            pl.BlockSpec((seq, hp * (NOPE_DIM + LANES)), lambda b, h, i: (b, h)),
            pl.BlockSpec((seq, hp * V_DIM), lambda b, h, i: (b, h)),
        ],
        out_specs=pl.BlockSpec((blk, hp * V_DIM), lambda b, h, i: (b * nb + i, h)),
        out_shape=jax.ShapeDtypeStruct((batch * seq, heads * V_DIM), BF16),
        scratch_shapes=[pltpu.VMEM((hp, blk, LANES), F32), pltpu.VMEM((hp, blk, LANES), F32),
                        pltpu.VMEM((hp, blk, V_DIM), F32)],
        compiler_params=_cparams(("parallel", "parallel", "arbitrary"), 32),
        name="flash",
    )(qn, qr, kcat, v)


def _absorb_body(qn_ref, w_ref, o_ref):
    o_ref[0] = _dot(qn_ref[...], w_ref[0]).astype(o_ref.dtype)


def _absorb(qn, w, *, row0, rows):
    heads, _, kv_rank = w.shape
    return pl.pallas_call(
        _absorb_body,
        grid=(heads,),
        in_specs=[
            pl.BlockSpec((rows, NOPE_DIM), lambda h: (row0 // rows, h)),
            pl.BlockSpec((1, NOPE_DIM, kv_rank), lambda h: (h, 0, 0)),
        ],
        out_specs=pl.BlockSpec((1, rows, kv_rank), lambda h: (h, 0, 0)),
        out_shape=jax.ShapeDtypeStruct((heads, rows, kv_rank), F32),
        compiler_params=_cparams(("parallel",), 16),
        name="absorb",
    )(qn, w)


def _v_up_body(o_ref_in, w_ref, o_ref):
    o_ref[...] = _dot(o_ref_in[0].astype(BF16), w_ref[0]).astype(o_ref.dtype)


def _v_up(o_lat, w):
    heads, rows, kv_rank = o_lat.shape
    return pl.pallas_call(
        _v_up_body,
        grid=(heads,),
        in_specs=[
            pl.BlockSpec((1, rows, kv_rank), lambda h: (h, 0, 0)),
            pl.BlockSpec((1, kv_rank, V_DIM), lambda h: (h, 0, 0)),
        ],
        out_specs=pl.BlockSpec((rows, V_DIM), lambda h: (0, h)),
        out_shape=jax.ShapeDtypeStruct((rows, heads * V_DIM), BF16),
        compiler_params=_cparams(("parallel",), 16),
        name="v_up",
    )(o_lat, w)


KEY_CHUNK = 512
PAGED_SPLITS = 1
PAGED_PAGES = 32


def _paged_body(pt_ref, qp_ref, qr_ref, cnew_ref, krnew_ref, wuk_ref, ones_ref, *rest, pages, heads, steps):
    c_refs = rest[:pages]
    krt_refs = rest[pages:2 * pages]
    o_ref, m_ref, l_ref, acc_ref, cpad_ref, krpad_ref = rest[2 * pages:]
    j = pl.program_id(1)
    nj = pl.num_programs(1)
    qrows = heads * steps
    qp = qp_ref[0].astype(BF16)
    qr = qr_ref[0].astype(BF16)

    @pl.when(j == 0)
    def _():
        m_ref[...] = jnp.full(m_ref.shape, -jnp.inf, F32)
        l_ref[...] = jnp.zeros(l_ref.shape, F32)
        acc_ref[...] = jnp.zeros(acc_ref.shape, F32)

    def attend(c_blk, krt_blk, valid):
        folded = None
        for t in range(wuk_ref.shape[1] // KEY_CHUNK):
            k = _dot(c_blk, wuk_ref[:, t * KEY_CHUNK:(t + 1) * KEY_CHUNK])
            sq = k * k
            for u in range(KEY_CHUNK // LANES):
                tile = sq[:, u * LANES:(u + 1) * LANES]
                folded = tile if folded is None else folded + tile
        hi = folded.astype(BF16)
        lo = (folded - hi.astype(F32)).astype(BF16)
        ssq = _dot(hi, ones_ref[...]) + _dot(lo, ones_ref[...])
        rinv_t = lax.rsqrt(ssq * (1.0 / NOPE_DIM) + EPS).T[:qrows, :]
        s = _nt(qp, c_blk) * rinv_t + _dot(qr, krt_blk)
        if valid is not None:
            s = jnp.where(valid, s, -jnp.inf)
        m_old = m_ref[...]
        m_new = jnp.maximum(m_old, jnp.max(s, axis=-1, keepdims=True))
        corr = jnp.exp2(m_old - m_new)
        p = jnp.exp2(s - m_new)
        l_ref[...] = l_ref[...] * corr + jnp.sum(p, axis=-1, keepdims=True)
        acc_ref[...] = acc_ref[...] * corr + _dot(p.astype(BF16), c_blk)
        m_ref[...] = m_new

    half = pages // PAGED_SPLITS
    for g in range(PAGED_SPLITS):
        ks = range(g * half, (g + 1) * half)
        c_blk = jnp.concatenate([c_refs[k][0, 0] for k in ks], axis=0).astype(BF16)
        krt_blk = jnp.concatenate([krt_refs[k][0, 0] for k in ks], axis=1).astype(BF16)
        attend(c_blk, krt_blk, None)

    @pl.when(j == nj - 1)
    def _():
        n = cpad_ref.shape[0]
        cpad_ref[...] = jnp.zeros(cpad_ref.shape, F32)
        krpad_ref[...] = jnp.zeros(krpad_ref.shape, F32)
        cpad_ref[0:steps, :] = cnew_ref[0]
        krpad_ref[0:steps, :] = krnew_ref[0]
        t = lax.broadcasted_iota(jnp.int32, (qrows, n), 0) % steps
        p = lax.broadcasted_iota(jnp.int32, (qrows, n), 1)
        krt_new = krpad_ref[...].T[:ROPE_DIM, :].astype(BF16)
        attend(cpad_ref[...].astype(BF16), krt_new, (p < steps) & (p <= t))
        o_ref[0] = acc_ref[...] / l_ref[...]


def _paged(page_table, qp, qr, c_new, kr_new, wuk, ones8, pool_c, pool_rt, *, layer, pages):
    nreq, qrows, kv_rank = qp.shape
    steps = c_new.shape[1]
    heads = qrows // steps
    n_pages = page_table.shape[1]
    assert qrows % SUBLANES == 0 and qrows <= LANES and n_pages % pages == 0 and pages % PAGED_SPLITS == 0
    flat_pt = page_table.reshape(-1)

    def page_spec(shape, k):
        return pl.BlockSpec((1, 1) + shape, lambda b, j, pt: (layer, pt[b * n_pages + j * pages + k], 0, 0))

    const = lambda shape: pl.BlockSpec(shape, lambda b, j, pt: (0,) * len(shape))
    grid_spec = pltpu.PrefetchScalarGridSpec(
        num_scalar_prefetch=1,
        grid=(nreq, n_pages // pages),
        in_specs=[
            pl.BlockSpec((1, qrows, kv_rank), lambda b, j, pt: (b, 0, 0)),
            pl.BlockSpec((1, qrows, ROPE_DIM), lambda b, j, pt: (b, 0, 0)),
            pl.BlockSpec((1, steps, kv_rank), lambda b, j, pt: (b, 0, 0)),
            pl.BlockSpec((1, steps, LANES), lambda b, j, pt: (b, 0, 0)),
            const(wuk.shape), const(ones8.shape),
        ] + [page_spec((PAGE_SIZE, kv_rank), k) for k in range(pages)]
          + [page_spec((ROPE_DIM, PAGE_SIZE), k) for k in range(pages)],
        out_specs=pl.BlockSpec((1, qrows, kv_rank), lambda b, j, pt: (b, 0, 0)),
        scratch_shapes=[
            pltpu.VMEM((qrows, 1), F32), pltpu.VMEM((qrows, 1), F32), pltpu.VMEM((qrows, kv_rank), F32),
            pltpu.VMEM((LANES, kv_rank), F32), pltpu.VMEM((LANES, LANES), F32),
        ],
    )
    return pl.pallas_call(
        functools.partial(_paged_body, pages=pages, heads=heads, steps=steps),
        grid_spec=grid_spec,
        out_shape=jax.ShapeDtypeStruct((nreq, qrows, kv_rank), F32),
        compiler_params=_cparams(("parallel", "arbitrary"), 48),
        name="paged",
    )(flat_pt, qp, qr, c_new, kr_new, wuk, ones8, *([pool_c] * pages), *([pool_rt] * pages))


def _rope_tables(positions):
    half = ROPE_DIM // 2
    inv = ROPE_BASE ** (-jnp.arange(half, dtype=F32) / half)
    ang = positions[:, None] * inv[None, :]
    cos = jnp.cos(ang).astype(F32)
    sin = jnp.sin(ang).astype(F32)
    reps = LANES // ROPE_DIM
    return jnp.tile(cos, (1, 2 * reps)), jnp.tile(jnp.concatenate([-sin, sin], axis=1), (1, reps))


def _slab(vec, lane0):
    return jnp.zeros((1, LANES), F32).at[0, lane0:lane0 + vec.shape[0]].set(vec.astype(F32))


def kernel(x_prompt, x_sample, cache_mla_latent, cache_mla_rope, state_ssm, state_conv, page_table, p_prompt, p_sample, norm_ffn1, w_ffn1_gate, w_ffn1_up, w_ffn1_down, norm_mix, w_in, gm_norm_v, gm_w_s, gm_b_s, ssm_conv_w, ssm_conv_b, ssm_dt_bias, ssm_a_log, ssm_d, ssm_norm, mla_q_norm, mla_w_uq, mla_qn_norm, mla_qr_norm, mla_kv_norm, mla_kr_norm, mla_w_uk, mla_kn_norm, mla_w_uv, w_out, norm_ffn2, w_ffn2_gate, w_ffn2_up, w_ffn2_down, norm_ple, w_ple_gate, w_ple_proj):
    bp, tp, d_model = x_prompt.shape
    bs, ts, _ = x_sample.shape
    depth = norm_ffn1.shape[0]
    mp, ms = bp * tp, bs * ts
    gm_width = gm_norm_v.shape[1]
    gm_heads = gm_width // GM_HEAD_DIM
    ssm_width = ssm_norm.shape[1]
    ssm_heads = ssm_width // SSM_HEAD_DIM
    conv_ch = ssm_conv_w.shape[2]
    q_rank = mla_q_norm.shape[1]
    kv_rank = mla_kv_norm.shape[1]
    mla_heads = mla_w_uq.shape[2]
    qscale = float((NOPE_DIM + ROPE_DIM) ** -0.5 * np.log2(np.e))

    tm = ms
    col_v = gm_width
    col_z = 2 * gm_width
    col_xbc = col_z + ssm_width
    col_cq = col_xbc + conv_ch
    col_ckv = col_cq + q_rank
    col_slab = col_ckv + kv_rank
    n_in = col_slab + LANES
    dt0 = col_xbc + conv_ch
    assert col_v == gm_width and col_z % ssm_width == 0 and col_xbc % conv_ch == 0
    assert col_cq % q_rank == 0 and col_ckv % kv_rank == 0 and col_slab % LANES == 0
    assert mp % tm == 0 and ms == tm and tp % SSM_CHUNK == 0 and ts < SUBLANES

    pos = jnp.concatenate([jnp.tile(jnp.arange(tp, dtype=F32), bp), jnp.tile(PAST_LEN + jnp.arange(ts, dtype=F32), bs)])
    cos_t, sin_s = _rope_tables(pos)
    bd = jnp.asarray(np.kron(np.eye(LANES // ROPE_DIM), np.ones((ROPE_DIM, ROPE_DIM))), BF16)
    tril = np.tril(np.ones((SSM_CHUNK, SSM_CHUNK)))
    tril3 = jnp.asarray(np.concatenate([tril] * 3, axis=1), BF16)
    e1 = np.zeros((LANES, ssm_width))
    for hh in range(ssm_heads):
        e1[DT_LANE0 + hh, hh * SSM_HEAD_DIM:(hh + 1) * SSM_HEAD_DIM] = 1.0
    e3 = jnp.asarray(np.concatenate([e1] * 3, axis=0), BF16)
    assert LANES % mla_heads == 0 and mla_heads * ts <= LANES
    ones8 = np.zeros((LANES, LANES))
    ones8[:, :mla_heads * ts] = np.kron(np.ones((LANES // mla_heads, 1)), np.kron(np.eye(mla_heads), np.ones((1, ts))))
    ones8 = jnp.asarray(ones8, BF16)
    pool_rt = jnp.swapaxes(cache_mla_rope, 2, 3)

    h = jnp.concatenate([x_prompt.reshape(mp, d_model), x_sample.reshape(ms, d_model)], axis=0)
    outs = [[] for _ in range(8)]
    ssm_all = state_ssm.astype(F32)
    n_prompt_blocks = mp // tm
    step_1hot = ((jnp.arange(ms) % ts)[:, None] == jnp.arange(ts)[None, :]).astype(F32)
    same_req = (jnp.arange(ms)[:, None] // ts) == (jnp.arange(ms)[None, :] // ts)

    for i in range(depth):
        bf = lambda w: w.astype(BF16)
        row = lambda v: v.reshape(1, -1).astype(F32)

        wgu = _cast_gate_up(w_ffn1_gate, w_ffn1_up, layer=i, tf=FFN_TF, tr=CAST_ROWS)
        h = _ffn(h, row(norm_ffn1[i]), wgu, _cast(w_ffn1_down, layer=i, tr=CAST_ROWS // 4), tm=tm)
        w_in_p = _cast_w_in(w_in, layer=i, dt0=dt0, n_dt=ssm_heads, n_out=n_in, tr=CAST_ROWS // 8)
        a = _norm_matmul(h, row(norm_mix[i]), w_in_p, tm=tm, tn=n_in // 5)

        ws = gm_w_s[i]
        wt = bf(jnp.tril(ws))
        w_prompt = jnp.zeros((gm_heads, tm, tm), BF16)
        for cc in range(tm // GM_CHUNK):
            w_prompt = w_prompt.at[:, cc * GM_CHUNK:(cc + 1) * GM_CHUNK, cc * GM_CHUNK:(cc + 1) * GM_CHUNK].set(wt)
        w_samp = jnp.einsum('rt,gts,cs->grc', step_1hot, jnp.tril(ws)[:, :ts, :ts], step_1hot,
                            precision=lax.Precision.HIGHEST)
        wbig = jnp.stack([w_prompt, bf(jnp.where(same_req, w_samp, 0.0))])
        bias_p = jnp.repeat(jnp.tile(gm_b_s[i].T, (tm // GM_CHUNK, 1)), GM_HEAD_DIM, axis=1)
        bias_s = jnp.repeat(jnp.tile(gm_b_s[i][:, :ts].T, (bs, 1)), GM_HEAD_DIM, axis=1)
        o_gm, vn = _sgu(a, row(gm_norm_v[i]), wbig, jnp.stack([bias_p, bias_s]).astype(F32),
                        rows=tm, n_prompt_blocks=n_prompt_blocks)

        ssd_params = (ssm_conv_w[i].astype(F32), row(ssm_conv_b[i]), _slab(ssm_dt_bias[i], DT_LANE0),
                      _slab(ssm_a_log[i], DT_LANE0), row(jnp.repeat(ssm_d[i], SSM_HEAD_DIM)), row(ssm_norm[i]),
                      tril3, e3)
        o_ssm_p, ssm_p, conv_p = _ssd_prompt(a, ssd_params, batch=bp, seq=tp,
                                             col_z=col_z, col_xbc=col_xbc, col_slab=col_slab)
        a_s = a[mp:]
        cs4 = jnp.pad(state_conv[i].astype(F32), ((0, 0), (0, ts - (SSM_CONV - 1)), (0, 0))).reshape(ms, conv_ch)
        o_ssm_s, ssm_all, conv_s = _ssd_sample(a_s, cs4, ssm_all, ssd_params, layer=i, steps=ts, nreq_blk=16,
                                             col_z=col_z, col_xbc=col_xbc, col_slab=col_slab)

        wuq = mla_w_uq[i]
        wuq_p = jnp.concatenate([wuq[:, :, :NOPE_DIM].reshape(q_rank, -1), wuq[:, :, NOPE_DIM:].reshape(q_rank, -1)], axis=1)
        mla_params = (row(mla_q_norm[i]), bf(wuq_p), row(mla_qn_norm[i]), row(jnp.tile(mla_qr_norm[i], LANES // ROPE_DIM)),
                      row(mla_kv_norm[i]), _slab(mla_kr_norm[i], 0), bd)
        qn, qr, c_lat, k_rope, kr2 = _mla_proj(a, cos_t, sin_s, mla_params, tm=tm,
                                               col_cq=col_cq, col_ckv=col_ckv, col_slab=col_slab, qscale=qscale)
        wuk = mla_w_uk[i]
        wuv = mla_w_uv[i]
        gkn = row(mla_kn_norm[i])
        wuk2 = bf(wuk.reshape(kv_rank, -1))
        k_p, v_p = _kv_proj(c_lat, kr2, wuk2, bf(wuv.reshape(kv_rank, -1)), gkn, rows=mp, tk=tm)
        o_mla_p = _flash(qn, qr, k_p, v_p, batch=bp, seq=tp, blk=tm)

        w_absorb = bf((wuk * mla_kn_norm[i][None, None, :]).transpose(1, 2, 0))
        qp = _absorb(qn, w_absorb, row0=mp, rows=ms)
        qp = qp.reshape(mla_heads, bs, ts, kv_rank).transpose(1, 0, 2, 3).reshape(bs, mla_heads * ts, kv_rank)
        qr_s = qr[mp:].astype(F32).reshape(bs, ts, mla_heads, ROPE_DIM).transpose(0, 2, 1, 3).reshape(bs, mla_heads * ts, ROPE_DIM)
        o_lat = _paged(page_table, qp, qr_s, c_lat[mp:].reshape(bs, ts, kv_rank), k_rope[mp:].reshape(bs, ts, LANES),
                       bf(wuk.transpose(0, 2, 1).reshape(kv_rank, -1)), ones8, cache_mla_latent, pool_rt,
                       layer=i, pages=min(PAGED_PAGES, page_table.shape[1]))
        o_lat = o_lat.reshape(bs, mla_heads, ts, kv_rank).transpose(1, 0, 2, 3).reshape(mla_heads, ms, kv_rank)
        o_mla_s = _v_up(o_lat, bf(wuv.transpose(1, 0, 2)))

        h = _out_proj(h, o_gm, o_ssm_p, o_ssm_s, o_mla_p, o_mla_s, _cast(w_out, layer=i, tr=CAST_ROWS // 4),
                      tm=tm, tn=1024)
        wgu = _cast_gate_up(w_ffn2_gate, w_ffn2_up, layer=i, tf=FFN_TF, tr=CAST_ROWS)
        h = _ffn(h, row(norm_ffn2[i]), wgu, _cast(w_ffn2_down, layer=i, tr=CAST_ROWS // 4), tm=tm)
        pe = jnp.concatenate([p_prompt[i].reshape(mp, -1), p_sample[i].reshape(ms, -1)], axis=0)
        h = _ple(h, row(norm_ple[i]), pe, _cast(w_ple_gate, layer=i, tr=CAST_ROWS // 4), bf(w_ple_proj[i]),
                 tm=tm, tn=1024)

        new = (c_lat[:mp].reshape(bp, tp, kv_rank), k_rope[:mp, :ROPE_DIM].reshape(bp, tp, ROPE_DIM),
               ssm_p, conv_p[:, SUBLANES - (SSM_CONV - 1):],
               c_lat[mp:].reshape(bs, ts, kv_rank), k_rope[mp:, :ROPE_DIM].reshape(bs, ts, ROPE_DIM),
               conv_s.reshape(bs, ts, conv_ch)[:, :SSM_CONV - 1], vn.reshape(bs, ts, gm_width))
        for lst, val in zip(outs, new):
            lst.append(val)

    stacked = [jnp.stack(o) for o in outs]
    stacked.insert(6, ssm_all.astype(state_ssm.dtype))
    return (h[:mp].reshape(bp, tp, d_model), h[mp:].reshape(bs, ts, d_model)) + tuple(stacked)
```
